```python
import math
import jax, jax.numpy as jnp
from jax import lax
import numpy as np

D_MODEL = 1024
BATCH = 1
SEQ = 16384
DEPTH = 4
DEC_BATCH = 16
DEC_SEQ = 2048
PAST_LEN = 128

D_CONV = 512
N_HEADS = 8
HEAD_DIM = 64
D_ATTN = N_HEADS * HEAD_DIM
D_IN = 2 * D_CONV + 3 * D_ATTN
CONV_WIDTH = 31
ATTN_BRANCHES = ((128, 1), (512, 4), (2048, 16))
ATTN_BLOCK = 64
N_BUCKETS = 32
MAX_DISTANCE = 1024
N_EXPERTS = 16
D_EXPERT = 2816
CAPACITY_FACTOR = 2
EPS = 1e-6
NEG = -1e30

kernel_name = "hymba_conformer_dilated_ec_encoder"


def _rmsnorm(x, g):
    xf = x.astype(jnp.float32)
    y = xf * lax.rsqrt(jnp.mean(xf * xf, axis=-1, keepdims=True) + EPS)
    return (y * g.astype(jnp.float32)).astype(x.dtype)


def _t5_bucket(rel):
    half = N_BUCKETS // 2
    max_exact = half // 2
    ret = np.where(rel > 0, half, 0)
    n = np.abs(rel)
    large = max_exact + (np.log(np.maximum(n, 1) / max_exact)
                         / np.log(MAX_DISTANCE / max_exact) * (half - max_exact)).astype(np.int32)
    large = np.minimum(large, half - 1)
    return (ret + np.where(n < max_exact, n, large)).astype(np.int32)


def _conformer_conv(val, gate, conv_w, conv_b, ln_g, ln_b):
    u = val * jax.nn.sigmoid(gate)
    pad = CONV_WIDTH // 2
    u = lax.conv_general_dilated(u, conv_w[:, None, :].astype(u.dtype), (1,), ((pad, pad),),
                                 dimension_numbers=('NWC', 'WIO', 'NWC'),
                                 feature_group_count=D_CONV) + conv_b.astype(u.dtype)
    uf = u.astype(jnp.float32)
    mu = jnp.mean(uf, axis=-1, keepdims=True)
    var = jnp.mean(jnp.square(uf - mu), axis=-1, keepdims=True)
    un = (uf - mu) * lax.rsqrt(var + EPS) * ln_g.astype(jnp.float32) + ln_b.astype(jnp.float32)
    return jax.nn.silu(un).astype(val.dtype)


def _dilated_branch(q, k, v, rel_bias, window, dilation):
    B, S, H, Dh = q.shape
    half = window // (2 * dilation)
    blk = ATTN_BLOCK
    L = S // dilation
    nb = -(-L // blk)
    Lp = nb * blk

    def to_blocks(t):
        t = t.reshape(B, L, dilation, H, Dh)
        t = jnp.pad(t, ((0, 0), (0, Lp - L), (0, 0), (0, 0), (0, 0)))
        return t.reshape(B, nb, blk, dilation, H, Dh)

    def neighbours(t):
        tp = jnp.pad(t, ((0, 0), (1, 1), (0, 0), (0, 0), (0, 0), (0, 0)))
        return jnp.concatenate([tp[:, :-2], tp[:, 1:-1], tp[:, 2:]], axis=2)

    qb = to_blocks(q)
    kw = neighbours(to_blocks(k))
    vw = neighbours(to_blocks(v))

    a = np.arange(blk)[:, None]
    b = np.arange(3 * blk)[None, :]
    delta = b - blk - a
    band = np.abs(delta) <= half
    key_m = (np.arange(nb)[:, None] - 1) * blk + np.arange(3 * blk)[None, :]
    mask = band[None] & ((key_m >= 0) & (key_m < L))[:, None, :]
    bias = jnp.transpose(rel_bias[_t5_bucket(dilation * delta)], (2, 0, 1)).astype(jnp.float32)

    s = jnp.einsum('bnqchd,bnkchd->bnchqk', qb, kw).astype(jnp.float32) * (Dh ** -0.5) + bias
    s = jnp.where(jnp.asarray(mask)[None, :, None, None], s, NEG)
    lse = jax.nn.logsumexp(s, axis=-1)
    p = jnp.exp(s - lse[..., None]).astype(v.dtype)
    o = jnp.einsum('bnchqk,bnkchd->bnqchd', p, vw)
    o = o.reshape(B, Lp, dilation, H, Dh)[:, :L].reshape(B, S, H, Dh)
    lse = jnp.transpose(lse, (0, 1, 4, 2, 3)).reshape(B, Lp, dilation, H)[:, :L].reshape(B, S, H)
    return o, lse


def _dilated_attention(q, k, v, rel_bias):
    outs, lses = [], []
    for window, dilation in ATTN_BRANCHES:
        o, l = _dilated_branch(q, k, v, rel_bias, window, dilation)
        outs.append(o)
        lses.append(l)
    w = jax.nn.softmax(jnp.stack(lses, axis=0), axis=0).astype(v.dtype)
    return jnp.einsum('gbsh,gbshd->bshd', w, jnp.stack(outs, axis=0))


def _mixer(xn, w_in, conv_w, conv_b, conv_ln_g, conv_ln_b, w_out, rel_bias):
    B, S, _ = xn.shape
    proj = xn @ w_in
    val, gate, q, k, v = jnp.split(proj, [D_CONV, 2 * D_CONV, 2 * D_CONV + D_ATTN,
                                          2 * D_CONV + 2 * D_ATTN], axis=-1)
    conv_out = _conformer_conv(val, gate, conv_w, conv_b, conv_ln_g, conv_ln_b)
    shp = (B, S, N_HEADS, HEAD_DIM)
    attn_out = _dilated_attention(q.reshape(shp), k.reshape(shp), v.reshape(shp), rel_bias)
    return jnp.concatenate([conv_out, attn_out.reshape(B, S, D_ATTN)], axis=-1) @ w_out


def _expert_choice_ffn(h, w_router, w_gate, w_up, w_down):
    B, S, D = h.shape
    n = B * S
    cap = CAPACITY_FACTOR * n // N_EXPERTS
    t = h.reshape(n, D)
    aff = jax.nn.softmax((t @ w_router).astype(jnp.float32), axis=-1)
    gates, idx = lax.top_k(aff.T, cap)
    xe = t[idx]
    hid = jax.nn.silu(jnp.einsum('ecd,edf->ecf', xe, w_gate)) * jnp.einsum('ecd,edf->ecf', xe, w_up)
    ye = jnp.einsum('ecf,efd->ecd', hid, w_down) * gates[..., None].astype(t.dtype)
    y = jnp.zeros((n, D), t.dtype).at[idx.reshape(-1)].add(ye.reshape(-1, D))
    return y.reshape(B, S, D)


def _trunk(x, norm_mix_g, w_in, conv_w, conv_b, conv_ln_g, conv_ln_b, w_out, rel_bias,
           norm_ffn_g, w_router, w_gate, w_up, w_down, norm_final_g):
    for l in range(DEPTH):
        x = x + _mixer(_rmsnorm(x, norm_mix_g[l]), w_in[l], conv_w[l], conv_b[l],
                       conv_ln_g[l], conv_ln_b[l], w_out[l], rel_bias)
        x = x + _expert_choice_ffn(_rmsnorm(x, norm_ffn_g[l]), w_router[l], w_gate[l],
                                   w_up[l], w_down[l])
    return _rmsnorm(x, norm_final_g)


def setup_inputs(seed: int = 0) -> dict:
    key = jax.random.key(seed)
    ks = jax.random.split(key, 18)
    f32 = jnp.float32
    nrm = lambda k, shape, scale: jax.random.normal(k, shape, f32) * scale
    return {
        "x_prompt": nrm(ks[0], (BATCH, SEQ, D_MODEL), 1.0),
        "x_sample": nrm(ks[1], (DEC_BATCH, DEC_SEQ, D_MODEL), 1.0),
        "norm_mix_g": 1.0 + nrm(ks[2], (DEPTH, D_MODEL), 0.02),
        "w_in": nrm(ks[3], (DEPTH, D_MODEL, D_IN), D_MODEL ** -0.5),
        "conv_w": nrm(ks[4], (DEPTH, CONV_WIDTH, D_CONV), CONV_WIDTH ** -0.5),
        "conv_b": nrm(ks[5], (DEPTH, D_CONV), 0.02),
        "conv_ln_g": 1.0 + nrm(ks[6], (DEPTH, D_CONV), 0.02),
        "conv_ln_b": nrm(ks[7], (DEPTH, D_CONV), 0.02),
        "w_out": nrm(ks[8], (DEPTH, D_CONV + D_ATTN, D_MODEL), (D_CONV + D_ATTN) ** -0.5),
        "rel_bias": nrm(ks[9], (N_BUCKETS, N_HEADS), 0.5),
        "norm_ffn_g": 1.0 + nrm(ks[10], (DEPTH, D_MODEL), 0.02),
        "w_router": nrm(ks[11], (DEPTH, D_MODEL, N_EXPERTS), D_MODEL ** -0.5),
        "w_gate": nrm(ks[12], (DEPTH, N_EXPERTS, D_MODEL, D_EXPERT), D_MODEL ** -0.5),
        "w_up": nrm(ks[13], (DEPTH, N_EXPERTS, D_MODEL, D_EXPERT), D_MODEL ** -0.5),
        "w_down": nrm(ks[14], (DEPTH, N_EXPERTS, D_EXPERT, D_MODEL), D_EXPERT ** -0.5),
        "norm_final_g": 1.0 + nrm(ks[15], (D_MODEL,), 0.02),
    }


def reference(x_prompt, x_sample, norm_mix_g, w_in, conv_w, conv_b, conv_ln_g, conv_ln_b,
              w_out, rel_bias, norm_ffn_g, w_router, w_gate, w_up, w_down, norm_final_g):
    y_prompt = _trunk(x_prompt, norm_mix_g, w_in, conv_w, conv_b, conv_ln_g, conv_ln_b, w_out,
                      rel_bias, norm_ffn_g, w_router, w_gate, w_up, w_down, norm_final_g)
    y_sample = _trunk(x_sample, norm_mix_g, w_in, conv_w, conv_b, conv_ln_g, conv_ln_b, w_out,
                      rel_bias, norm_ffn_g, w_router, w_gate, w_up, w_down, norm_final_g)
    return (y_prompt, y_sample)
```

```python
import functools

import jax
import jax.numpy as jnp
import numpy as np
from jax import lax
from jax.experimental import pallas as pl
from jax.experimental.pallas import tpu as pltpu

F32 = jnp.float32
BF16 = jnp.bfloat16
I32 = jnp.int32

D_MODEL = 1024
D_CONV = 512
N_HEADS = 8
HEAD_DIM = 64
D_ATTN = N_HEADS * HEAD_DIM
CONV_WIDTH = 31
ATTN_DILATIONS = (1, 4, 16)
ATTN_HALF = 64
N_BUCKETS = 32
MAX_DISTANCE = 1024
N_EXPERTS = 16
CAPACITY_FACTOR = 2
EPS = 1e-6
NEG = -1e30

LANES = 128
QGROUP = 128
KGROUP = QGROUP + 2 * ATTN_HALF
CHUNK = QGROUP * max(ATTN_DILATIONS)
HALO = ATTN_HALF * max(ATTN_DILATIONS)
CONV_PAD = 16
ROW_TILE = 512
FFN_ROWS = 1024
FFN_COLS = 256
SLOT_BLOCK = 512
VMEM_LIMIT = 56 * 1024 * 1024


def _params(*sem):
    return pltpu.CompilerParams(dimension_semantics=sem, vmem_limit_bytes=VMEM_LIMIT)


def _sigmoid(x):
    return 1.0 / (1.0 + jnp.exp(-x))


def _mixer_in_body(x_ref, g_ref, w_ref, u_ref, q_ref, k_ref, v_ref):
    x = x_ref[...]
    xn = (x * lax.rsqrt(jnp.mean(x * x, axis=-1, keepdims=True) + EPS) * g_ref[...]).astype(BF16)

    def proj(j, width):
        return jnp.dot(xn, w_ref[:, j:j + width], preferred_element_type=F32)

    val = proj(0, D_CONV)
    gate = proj(D_CONV, D_CONV)
    u_ref[...] = val * _sigmoid(gate)
    q_ref[...] = proj(2 * D_CONV, D_ATTN) * (HEAD_DIM ** -0.5)
    k_ref[...] = proj(2 * D_CONV + D_ATTN, D_ATTN)
    v_ref[...] = proj(2 * D_CONV + 2 * D_ATTN, D_ATTN)


def _mixer_in(x, g, w_in_bf16):
    n = x.shape[0]
    d_in = w_in_bf16.shape[1]
    row = lambda w: pl.BlockSpec((ROW_TILE, w), lambda i: (i, 0))
    full = lambda a: pl.BlockSpec(a.shape, lambda i: (0,) * a.ndim)
    out = lambda w: jax.ShapeDtypeStruct((n, w), F32)
    return pl.pallas_call(
        _mixer_in_body,
        grid=(n // ROW_TILE,),
        in_specs=[row(D_MODEL), full(g), full(w_in_bf16)],
        out_specs=[row(D_CONV), row(D_ATTN), row(D_ATTN), row(D_ATTN)],
        out_shape=[out(D_CONV), out(D_ATTN), out(D_ATTN), out(D_ATTN)],
        compiler_params=_params("parallel"),
        name="mixer_in",
    )(x, g, w_in_bf16)


CONV_ROWS = 64
CONV_LANES = 256


def _conv_body(hp_ref, hn_ref, up_ref, uc_ref, un_ref, w_ref, b_ref, g_ref, b2_ref, o_ref, win_ref):
    c = pl.program_id(0)
    keep_prev = (hp_ref[c] > 0).astype(F32)
    keep_next = (hn_ref[c] > 0).astype(F32)
    win_ref[0:CONV_PAD, :] = up_ref[...] * keep_prev
    win_ref[CONV_PAD:CONV_PAD + CHUNK, :] = uc_ref[...]
    win_ref[CONV_PAD + CHUNK:, :] = un_ref[...] * keep_next
    first = CONV_PAD - CONV_WIDTH // 2

    def tile(i, carry):
        r0 = pl.multiple_of(i * CONV_ROWS, CONV_ROWS)
        halves = []
        for c0 in range(0, D_CONV, CONV_LANES):
            win = win_ref[pl.ds(r0, CONV_ROWS + 2 * CONV_PAD), c0:c0 + CONV_LANES]
            acc = jnp.zeros((CONV_ROWS, CONV_LANES), F32)
            for k in range(CONV_WIDTH):
                acc = acc + win[first + k:first + k + CONV_ROWS, :] * w_ref[k:k + 1, c0:c0 + CONV_LANES]
            halves.append(acc)
        y = jnp.concatenate(halves, axis=1) + b_ref[...]
        mu = jnp.mean(y, axis=-1, keepdims=True)
        yc = y - mu
        var = jnp.mean(yc * yc, axis=-1, keepdims=True)
        yn = yc * lax.rsqrt(var + EPS) * g_ref[...] + b2_ref[...]
        o_ref[pl.ds(r0, CONV_ROWS), :] = (yn * _sigmoid(yn)).astype(BF16)
        return carry

    lax.fori_loop(0, CHUNK // CONV_ROWS, tile, 0)


def _conformer_conv(u, has_prev, has_next, conv_w, conv_b, ln_g, ln_b):
    n = u.shape[0]
    n_chunks = n // CHUNK
    per = CHUNK // CONV_PAD
    last = n // CONV_PAD - 1
    full = lambda a: pl.BlockSpec(a.shape, lambda c, hp, hn: (0,) * a.ndim)
    grid_spec = pltpu.PrefetchScalarGridSpec(
        num_scalar_prefetch=2,
        grid=(n_chunks,),
        in_specs=[
            pl.BlockSpec((CONV_PAD, D_CONV), lambda c, hp, hn: (jnp.maximum(c * per - 1, 0), 0)),
            pl.BlockSpec((CHUNK, D_CONV), lambda c, hp, hn: (c, 0)),
            pl.BlockSpec((CONV_PAD, D_CONV), lambda c, hp, hn: (jnp.minimum((c + 1) * per, last), 0)),
            full(conv_w), full(conv_b), full(ln_g), full(ln_b),
        ],
        out_specs=pl.BlockSpec((CHUNK, D_CONV), lambda c, hp, hn: (c, 0)),
        scratch_shapes=[pltpu.VMEM((CHUNK + 2 * CONV_PAD, D_CONV), F32)],
    )
    return pl.pallas_call(
        _conv_body,
        grid_spec=grid_spec,
        out_shape=jax.ShapeDtypeStruct((n, D_CONV), BF16),
        compiler_params=_params("parallel"),
        name="conformer_conv",
    )(has_prev, has_next, u, u, u, conv_w, conv_b, ln_g, ln_b)


def _t5_bucket(rel):
    half = N_BUCKETS // 2
    max_exact = half // 2
    ret = np.where(rel > 0, half, 0)
    n = np.abs(rel)
    large = max_exact + (np.log(np.maximum(n, 1) / max_exact)
                         / np.log(MAX_DISTANCE / max_exact) * (half - max_exact)).astype(np.int32)
    large = np.minimum(large, half - 1)
    return (ret + np.where(n < max_exact, n, large)).astype(np.int32)


def _bias_tables(rel_bias):
    i = np.arange(QGROUP)[:, None]
    j = np.arange(KGROUP)[None, :]
    delta = j - ATTN_HALF - i
    band = np.abs(delta) <= ATTN_HALF
    tabs = []
    for d in ATTN_DILATIONS:
        b = rel_bias[_t5_bucket(d * delta)].astype(F32)
        b = jnp.where(band[:, :, None], b, NEG)
        b = jnp.transpose(b, (2, 0, 1))
        tabs.append(b.reshape(N_HEADS // 2, 2 * QGROUP, KGROUP))
    return jnp.stack(tabs, axis=0)


def _attn_body(hp_ref, hn_ref, q_ref, kp_ref, kc_ref, kn_ref, vp_ref, vc_ref, vn_ref, tab_ref, o_ref,
               kw_ref, vw_ref, oacc_ref, macc_ref, lacc_ref):
    c = pl.program_id(0)
    has_prev = hp_ref[c] > 0
    has_next = hn_ref[c] > 0
    kw_ref[0:HALO, :] = kp_ref[...]
    kw_ref[HALO:HALO + CHUNK, :] = kc_ref[...]
    kw_ref[HALO + CHUNK:, :] = kn_ref[...]
    vw_ref[0:HALO, :] = vp_ref[...]
    vw_ref[HALO:HALO + CHUNK, :] = vc_ref[...]
    vw_ref[HALO + CHUNK:, :] = vn_ref[...]
    oacc_ref[...] = jnp.zeros_like(oacc_ref)
    lacc_ref[...] = jnp.zeros_like(lacc_ref)
    macc_ref[...] = jnp.full_like(macc_ref, NEG)

    first_head = lax.broadcasted_iota(I32, (QGROUP, LANES), 1) < HEAD_DIM
    key_pos = lax.broadcasted_iota(I32, (1, KGROUP), 1)

    for bi, d in enumerate(ATTN_DILATIONS):
        stride = None if d == 1 else d

        def group(g, carry, bi=bi, d=d, stride=stride):
            res = g % d
            blk = g // d
            q0 = blk * (QGROUP * d) + res
            k0 = HALO + (blk * QGROUP - ATTN_HALF) * d + res
            q = q_ref[pl.ds(q0, QGROUP, stride=stride), :]
            k = kw_ref[pl.ds(k0, KGROUP, stride=stride), :]
            v = vw_ref[pl.ds(k0, KGROUP, stride=stride), :]
            row = k0 + d * key_pos
            valid = ((row >= HALO) | has_prev) & ((row < HALO + CHUNK) | has_next)
            q2 = jnp.concatenate([jnp.where(first_head, q, 0.0), jnp.where(first_head, 0.0, q)], axis=0)
            s = lax.dot_general(q2.astype(BF16), k.astype(BF16), (((1,), (1,)), ((), ())),
                                preferred_element_type=F32)
            s = jnp.where(valid, s + tab_ref[bi, 0], NEG)
            m = jnp.max(s, axis=-1, keepdims=True)
            p = jnp.exp(s - m)
            l = jnp.sum(p, axis=-1, keepdims=True)
            o = jnp.dot(p.astype(BF16), v.astype(BF16), preferred_element_type=F32)
            o_new = jnp.where(first_head, o[:QGROUP], o[QGROUP:])
            m_new = jnp.where(first_head, m[:QGROUP], m[QGROUP:])
            l_new = jnp.where(first_head, l[:QGROUP], l[QGROUP:])
            rows = pl.ds(q0, QGROUP, stride=stride)
            m_old = macc_ref[rows, :]
            m_tot = jnp.maximum(m_old, m_new)
            a_old = jnp.exp(m_old - m_tot)
            a_new = jnp.exp(m_new - m_tot)
            oacc_ref[rows, :] = a_old * oacc_ref[rows, :] + a_new * o_new
            lacc_ref[rows, :] = a_old * lacc_ref[rows, :] + a_new * l_new
            macc_ref[rows, :] = m_tot
            return carry

        lax.fori_loop(0, CHUNK // QGROUP, group, 0)

    o_ref[...] = (oacc_ref[...] / lacc_ref[...]).astype(BF16)


def _dilated_attention(q, k, v, has_prev, has_next, tables):
    n = q.shape[0]
    n_chunks = n // CHUNK
    per = CHUNK // HALO
    last = n // HALO - 1
    cur = pl.BlockSpec((CHUNK, LANES), lambda c, h, hp, hn: (c, h))
    prev = pl.BlockSpec((HALO, LANES), lambda c, h, hp, hn: (jnp.maximum(c * per - 1, 0), h))
    nxt = pl.BlockSpec((HALO, LANES), lambda c, h, hp, hn: (jnp.minimum((c + 1) * per, last), h))
    grid_spec = pltpu.PrefetchScalarGridSpec(
        num_scalar_prefetch=2,
        grid=(n_chunks, D_ATTN // LANES),
        in_specs=[cur, prev, cur, nxt, prev, cur, nxt,
                  pl.BlockSpec((len(ATTN_DILATIONS), 1, 2 * QGROUP, KGROUP), lambda c, h, hp, hn: (0, h, 0, 0))],
        out_specs=cur,
        scratch_shapes=[pltpu.VMEM((CHUNK + 2 * HALO, LANES), F32), pltpu.VMEM((CHUNK + 2 * HALO, LANES), F32),
                        pltpu.VMEM((CHUNK, LANES), F32), pltpu.VMEM((CHUNK, LANES), F32),
                        pltpu.VMEM((CHUNK, LANES), F32)],
    )
    return pl.pallas_call(
        _attn_body,
        grid_spec=grid_spec,
        out_shape=jax.ShapeDtypeStruct((n, D_ATTN), BF16),
        compiler_params=_params("parallel", "parallel"),
        name="dilated_attention",
    )(has_prev, has_next, q, k, k, k, v, v, v, tables)


def _mixer_out_body(cv_ref, at_ref, x_ref, w_ref, g_ref, wr_ref, xo_ref, h_ref, aff_ref):
    y = (jnp.dot(cv_ref[...], w_ref[0:D_CONV, :], preferred_element_type=F32)
         + jnp.dot(at_ref[...], w_ref[D_CONV:, :], preferred_element_type=F32))
    x = x_ref[...] + y
    xo_ref[...] = x
    h = x * lax.rsqrt(jnp.mean(x * x, axis=-1, keepdims=True) + EPS) * g_ref[...]
    h_ref[...] = h
    logits = lax.dot_general(wr_ref[...], h, (((1,), (1,)), ((), ())),
                             precision=lax.Precision.HIGHEST, preferred_element_type=F32)
    e = jnp.exp(logits - jnp.max(logits, axis=0, keepdims=True))
    aff_ref[...] = e / jnp.sum(e, axis=0, keepdims=True)


def _mixer_out(conv_out, attn_out, x, w_out_bf16, g, w_router_t):
    n = x.shape[0]
    row = lambda w: pl.BlockSpec((ROW_TILE, w), lambda i: (i, 0))
    full = lambda a: pl.BlockSpec(a.shape, lambda i: (0,) * a.ndim)
    return pl.pallas_call(
        _mixer_out_body,
        grid=(n // ROW_TILE,),
        in_specs=[row(D_CONV), row(D_ATTN), row(D_MODEL), full(w_out_bf16), full(g), full(w_router_t)],
        out_specs=[row(D_MODEL), row(D_MODEL), pl.BlockSpec((N_EXPERTS, ROW_TILE), lambda i: (0, i))],
        out_shape=[jax.ShapeDtypeStruct((n, D_MODEL), F32), jax.ShapeDtypeStruct((n, D_MODEL), F32),
                   jax.ShapeDtypeStruct((N_EXPERTS, n), F32)],
        input_output_aliases={2: 0},
        compiler_params=_params("parallel"),
        name="mixer_out_router",
    )(conv_out, attn_out, x, w_out_bf16, g, w_router_t)


def _select_body(aff_ref, idx_ref, gate_ref, *, cap, tok0):
    a = aff_ref[0]
    segs = a.shape[0]
    bits = pltpu.bitcast(a, I32)

    def count(mask):
        return jnp.sum(jnp.sum(mask.astype(F32), axis=0, keepdims=True), axis=1, keepdims=True)

    thr = jnp.zeros((1, 1), I32)
    for bit in range(30, -1, -1):
        cand = thr | (1 << bit)
        thr = jnp.where(count(bits >= cand) >= cap, cand, thr)
    above = bits > thr
    equal = bits == thr
    need = cap - count(above)

    lane_r = lax.broadcasted_iota(I32, (LANES, LANES), 0)
    lane_c = lax.broadcasted_iota(I32, (LANES, LANES), 1)
    upper = (lane_r <= lane_c).astype(BF16)
    seg_r = lax.broadcasted_iota(I32, (segs, segs), 0)
    seg_c = lax.broadcasted_iota(I32, (segs, segs), 1)
    before = (seg_c < seg_r).astype(BF16)

    def prefix(mask):
        incl = jnp.dot(mask.astype(BF16), upper, preferred_element_type=F32)
        tot = jnp.broadcast_to(incl[:, LANES - 1:LANES], (segs, LANES))
        pre = jnp.dot(before, tot.astype(BF16), preferred_element_type=F32)
        return incl, pre, tot

    eq_f = equal.astype(F32)
    incl_e, pre_e, _ = prefix(eq_f)
    chosen = above | (equal & (pre_e + incl_e - eq_f < need))
    incl, pre, tot = prefix(chosen.astype(F32))
    seg_lo = pre[:, 0:1]
    seg_hi = seg_lo + tot[:, 0:1]
    incl_t = jnp.transpose(incl).astype(BF16)
    aff_t = jnp.transpose(a)
    lane_id = lax.broadcasted_iota(I32, (LANES, SLOT_BLOCK), 0).astype(F32)

    for jb in range(cap // SLOT_BLOCK):
        slot = (jb * SLOT_BLOCK + lax.broadcasted_iota(I32, (1, SLOT_BLOCK), 1)).astype(F32)
        seg_of = jnp.sum((seg_hi <= slot).astype(F32), axis=0, keepdims=True)
        onehot = ((seg_lo <= slot) & (slot < seg_hi)).astype(F32)
        rank = slot - jnp.sum(onehot * seg_lo, axis=0, keepdims=True)
        counts = jnp.dot(incl_t, onehot.astype(BF16), preferred_element_type=F32)
        lane_of = jnp.sum((counts <= rank).astype(F32), axis=0, keepdims=True)
        idx_ref[0, :, jb * SLOT_BLOCK:(jb + 1) * SLOT_BLOCK] = (seg_of * LANES + lane_of).astype(I32) + tok0
        vals = jnp.dot(aff_t, onehot, precision=lax.Precision.HIGHEST, preferred_element_type=F32)
        gate_ref[0, :, jb * SLOT_BLOCK:(jb + 1) * SLOT_BLOCK] = jnp.sum(
            jnp.where(lane_id == lane_of, vals, 0.0), axis=0, keepdims=True)


def _select(aff_group, cap, tok0):
    n = aff_group.shape[1]
    aff3 = aff_group.reshape(N_EXPERTS, n // LANES, LANES)
    spec = pl.BlockSpec((1, 1, cap), lambda e: (e, 0, 0))
    return pl.pallas_call(
        functools.partial(_select_body, cap=cap, tok0=tok0),
        grid=(N_EXPERTS,),
        in_specs=[pl.BlockSpec((1, n // LANES, LANES), lambda e: (e, 0, 0))],
        out_specs=[spec, spec],
        out_shape=[jax.ShapeDtypeStruct((N_EXPERTS, 1, cap), I32), jax.ShapeDtypeStruct((N_EXPERTS, 1, cap), F32)],
        compiler_params=_params("parallel"),
        name="expert_select",
    )(aff3)


def _row_copy(src_hbm, dst_hbm, tok, buf, j, sem, to_hbm):
    if to_hbm:
        return pltpu.make_async_copy(buf.at[pl.ds(j, 1), :], dst_hbm.at[pl.ds(tok, 1), :], sem)
    return pltpu.make_async_copy(src_hbm.at[pl.ds(tok, 1), :], buf.at[pl.ds(j, 1), :], sem)


def _ffn_body(idx_ref, h_hbm, gate_ref, wg_ref, wu_ref, wd_ref, x_in_hbm, x_hbm,
              hbuf_ref, xb_ref, xrows_ref, acc_ref, sem_h, sem_x, sem_s, *, rows, tiles, steps):
    del x_in_hbm
    e = pl.program_id(0)
    t = pl.program_id(1)
    f = pl.program_id(2)
    base = (e * tiles + t) * rows

    @pl.when(f == 0)
    def _():
        def issue(j, carry):
            tok = idx_ref[base + j]
            _row_copy(h_hbm, None, tok, hbuf_ref, j, sem_h, False).start()
            _row_copy(x_hbm, None, tok, xrows_ref, j, sem_x, False).start()
            return carry
        lax.fori_loop(0, rows, issue, 0)
        pltpu.make_async_copy(h_hbm.at[pl.ds(0, rows), :], hbuf_ref, sem_h).wait()
        xb_ref[...] = hbuf_ref[...].astype(BF16)
        acc_ref[...] = jnp.zeros_like(acc_ref)

    xb = xb_ref[...]
    hg = jnp.dot(xb, wg_ref[0].astype(BF16), preferred_element_type=F32)
    hu = jnp.dot(xb, wu_ref[0].astype(BF16), preferred_element_type=F32)
    hid = (hg * _sigmoid(hg) * hu).astype(BF16)
    acc_ref[...] += jnp.dot(hid, wd_ref[0].astype(BF16), preferred_element_type=F32)

    @pl.when(f == steps - 1)
    def _():
        pltpu.make_async_copy(x_hbm.at[pl.ds(0, rows), :], xrows_ref, sem_x).wait()
        xrows_ref[...] = xrows_ref[...] + acc_ref[...] * gate_ref[0]

        def issue(j, carry):
            tok = idx_ref[base + j]
            _row_copy(None, x_hbm, tok, xrows_ref, j, sem_s, True).start()
            return carry
        lax.fori_loop(0, rows, issue, 0)
        pltpu.make_async_copy(xrows_ref, x_hbm.at[pl.ds(0, rows), :], sem_s).wait()


def _expert_ffn(idx_flat, h, gates, w_gate, w_up, w_down, x):
    n_slots = gates.shape[1]
    d_exp = w_gate.shape[2]
    tiles = n_slots // FFN_ROWS
    steps = d_exp // FFN_COLS
    any_spec = pl.BlockSpec(memory_space=pl.ANY)
    grid_spec = pltpu.PrefetchScalarGridSpec(
        num_scalar_prefetch=1,
        grid=(N_EXPERTS, tiles, steps),
        in_specs=[
            any_spec,
            pl.BlockSpec((1, FFN_ROWS, 1), lambda e, t, f, idx: (e, t, 0)),
            pl.BlockSpec((1, D_MODEL, FFN_COLS), lambda e, t, f, idx: (e, 0, f)),
            pl.BlockSpec((1, D_MODEL, FFN_COLS), lambda e, t, f, idx: (e, 0, f)),
            pl.BlockSpec((1, FFN_COLS, D_MODEL), lambda e, t, f, idx: (e, f, 0)),
            any_spec,
        ],
        out_specs=any_spec,
        scratch_shapes=[pltpu.VMEM((FFN_ROWS, D_MODEL), F32), pltpu.VMEM((FFN_ROWS, D_MODEL), BF16),
                        pltpu.VMEM((FFN_ROWS, D_MODEL), F32), pltpu.VMEM((FFN_ROWS, D_MODEL), F32),
                        pltpu.SemaphoreType.DMA, pltpu.SemaphoreType.DMA, pltpu.SemaphoreType.DMA],
    )
    return pl.pallas_call(
        functools.partial(_ffn_body, rows=FFN_ROWS, tiles=tiles, steps=steps),
        grid_spec=grid_spec,
        out_shape=jax.ShapeDtypeStruct(x.shape, F32),
        input_output_aliases={6: 0},
        compiler_params=_params("arbitrary", "arbitrary", "arbitrary"),
        name="expert_ffn",
    )(idx_flat, h, gates, w_gate, w_up, w_down, x)


def _final_norm_body(x_ref, g_ref, o_ref):
    x = x_ref[...]
    o_ref[...] = x * lax.rsqrt(jnp.mean(x * x, axis=-1, keepdims=True) + EPS) * g_ref[...]


def _final_norm(x, g):
    n = x.shape[0]
    row = pl.BlockSpec((ROW_TILE, D_MODEL), lambda i: (i, 0))
    return pl.pallas_call(
        _final_norm_body,
        grid=(n // ROW_TILE,),
        in_specs=[row, pl.BlockSpec(g.shape, lambda i: (0, 0))],
        out_specs=row,
        out_shape=jax.ShapeDtypeStruct((n, D_MODEL), F32),
        compiler_params=_params("parallel"),
        name="final_norm",
    )(x, g)


def _chunk_neighbours(seq_lens):
    has_prev, has_next = [], []
    for length in seq_lens:
        assert length % CHUNK == 0
        k = length // CHUNK
        has_prev += [int(i > 0) for i in range(k)]
        has_next += [int(i < k - 1) for i in range(k)]
    return jnp.asarray(has_prev, I32), jnp.asarray(has_next, I32)


def _trunk(x, groups, norm_mix_g, w_in, conv_w, conv_b, conv_ln_g, conv_ln_b, w_out, rel_bias,
           norm_ffn_g, w_router, w_gate, w_up, w_down, norm_final_g):
    seq_lens = [s for b, s in groups for _ in range(b)]
    has_prev, has_next = _chunk_neighbours(seq_lens)
    tables = _bias_tables(rel_bias)
    depth = w_in.shape[0]
    row2 = lambda a: a.reshape(1, -1)
    for l in range(depth):
        u, q, k, v = _mixer_in(x, row2(norm_mix_g[l]), w_in[l].astype(BF16))
        conv_out = _conformer_conv(u, has_prev, has_next, conv_w[l], row2(conv_b[l]),
                                   row2(conv_ln_g[l]), row2(conv_ln_b[l]))
        attn_out = _dilated_attention(q, k, v, has_prev, has_next, tables)
        x, h, aff = _mixer_out(conv_out, attn_out, x, w_out[l].astype(BF16), row2(norm_ffn_g[l]),
                               jnp.transpose(w_router[l]))
        idx, gates, tok0 = [], [], 0
        for b, s in groups:
            n = b * s
            cap = CAPACITY_FACTOR * n // N_EXPERTS
            i, g = _select(aff[:, tok0:tok0 + n], cap, tok0)
            idx.append(i[:, 0, :])
            gates.append(g[:, 0, :])
            tok0 += n
        idx = jnp.concatenate(idx, axis=1)
        gates = jnp.concatenate(gates, axis=1)[:, :, None]
        x = _expert_ffn(idx.reshape(-1), h, gates, w_gate[l], w_up[l], w_down[l], x)
    return _final_norm(x, row2(norm_final_g))


def kernel(x_prompt, x_sample, norm_mix_g, w_in, conv_w, conv_b, conv_ln_g, conv_ln_b, w_out, rel_bias,
           norm_ffn_g, w_router, w_gate, w_up, w_down, norm_final_g):
    groups = [x_prompt.shape[:2], x_sample.shape[:2]]
    x = jnp.concatenate([x_prompt.reshape(-1, D_MODEL), x_sample.reshape(-1, D_MODEL)], axis=0)
    y = _trunk(x, groups, norm_mix_g, w_in, conv_w, conv_b, conv_ln_g, conv_ln_b, w_out, rel_bias,
               norm_ffn_g, w_router, w_gate, w_up, w_down, norm_final_g)
    n_prompt = x_prompt.shape[0] * x_prompt.shape[1]
    return y[:n_prompt].reshape(x_prompt.shape), y[n_prompt:].reshape(x_sample.shape)
```

```python
import functools

import jax
import jax.numpy as jnp
import numpy as np
from jax import lax
from jax.experimental import pallas as pl
from jax.experimental.pallas import tpu as pltpu

F32 = jnp.float32
BF16 = jnp.bfloat16
I32 = jnp.int32

D_MODEL = 1024
D_CONV = 512
N_HEADS = 8
HEAD_DIM = 64
D_ATTN = N_HEADS * HEAD_DIM
CONV_WIDTH = 31
ATTN_DILATIONS = (1, 4, 16)
ATTN_HALF = 64
N_BUCKETS = 32
MAX_DISTANCE = 1024
N_EXPERTS = 16
CAPACITY_FACTOR = 2
EPS = 1e-6
NEG = -1e30

LANES = 128
QGROUP = 128
KGROUP = QGROUP + 2 * ATTN_HALF
CHUNK = QGROUP * max(ATTN_DILATIONS)
HALO = ATTN_HALF * max(ATTN_DILATIONS)
CONV_PAD = 16
ROW_TILE = 512
FFN_ROWS = 1024
FFN_COLS = 256
SLOT_BLOCK = 512
VMEM_LIMIT = 56 * 1024 * 1024


def _params(*sem):
    return pltpu.CompilerParams(dimension_semantics=sem, vmem_limit_bytes=VMEM_LIMIT)


def _sigmoid(x):
    return 1.0 / (1.0 + jnp.exp(-x))


SLABS = D_MODEL // LANES


def _load_tokens(ref, n):
    return jnp.concatenate([ref[pl.ds(s, n, stride=SLABS), :] for s in range(SLABS)], axis=1)


def _store_tokens(ref, val):
    n = val.shape[0]
    for s in range(SLABS):
        ref[pl.ds(s, n, stride=SLABS), :] = val[:, s * LANES:(s + 1) * LANES]


def _tiled_spec(rows, first=0):
    return pl.BlockSpec((SLABS * rows, LANES), lambda i, *_: (i + first, 0))


def _row_tiled_body(xa_ref, xb_ref, o_ref, *, tiles_a):
    i = pl.program_id(0)

    @pl.when(i < tiles_a)
    def _():
        _store_tokens(o_ref, xa_ref[...])

    @pl.when(i >= tiles_a)
    def _():
        _store_tokens(o_ref, xb_ref[...])


def _row_tiled_concat(xa, xb):
    tiles_a = xa.shape[0] // ROW_TILE
    tiles_b = xb.shape[0] // ROW_TILE
    n = xa.shape[0] + xb.shape[0]
    return pl.pallas_call(
        functools.partial(_row_tiled_body, tiles_a=tiles_a),
        grid=(tiles_a + tiles_b,),
        in_specs=[pl.BlockSpec((ROW_TILE, D_MODEL), lambda i: (jnp.minimum(i, tiles_a - 1), 0)),
                  pl.BlockSpec((ROW_TILE, D_MODEL), lambda i: (jnp.maximum(i - tiles_a, 0), 0))],
        out_specs=_tiled_spec(ROW_TILE),
        out_shape=jax.ShapeDtypeStruct((SLABS * n, LANES), F32),
        compiler_params=_params("parallel"),
        name="row_tiled_concat",
    )(xa, xb)


def _mixer_in_body(x_ref, g_ref, w_ref, u_ref, q_ref, k_ref, v_ref):
    x = _load_tokens(x_ref, ROW_TILE)
    xn = (x * lax.rsqrt(jnp.mean(x * x, axis=-1, keepdims=True) + EPS) * g_ref[...]).astype(BF16)

    def proj(j, width):
        return jnp.dot(xn, w_ref[:, j:j + width], preferred_element_type=F32)

    val = proj(0, D_CONV)
    gate = proj(D_CONV, D_CONV)
    u_ref[...] = val * _sigmoid(gate)
    q_ref[...] = proj(2 * D_CONV, D_ATTN) * (HEAD_DIM ** -0.5)
    k_ref[...] = proj(2 * D_CONV + D_ATTN, D_ATTN)
    v_ref[...] = proj(2 * D_CONV + 2 * D_ATTN, D_ATTN)


def _mixer_in(xt, g, w_in_bf16):
    n = xt.shape[0] // SLABS
    row = lambda w: pl.BlockSpec((ROW_TILE, w), lambda i: (i, 0))
    full = lambda a: pl.BlockSpec(a.shape, lambda i: (0,) * a.ndim)
    out = lambda w: jax.ShapeDtypeStruct((n, w), F32)
    return pl.pallas_call(
        _mixer_in_body,
        grid=(n // ROW_TILE,),
        in_specs=[_tiled_spec(ROW_TILE), full(g), full(w_in_bf16)],
        out_specs=[row(D_CONV), row(D_ATTN), row(D_ATTN), row(D_ATTN)],
        out_shape=[out(D_CONV), out(D_ATTN), out(D_ATTN), out(D_ATTN)],
        compiler_params=_params("parallel"),
        name="mixer_in",
    )(xt, g, w_in_bf16)


CONV_ROWS = 64
CONV_LANES = 256


def _conv_body(hp_ref, hn_ref, up_ref, uc_ref, un_ref, w_ref, b_ref, g_ref, b2_ref, o_ref, win_ref):
    c = pl.program_id(0)
    keep_prev = (hp_ref[c] > 0).astype(F32)
    keep_next = (hn_ref[c] > 0).astype(F32)
    win_ref[0:CONV_PAD, :] = up_ref[...] * keep_prev
    win_ref[CONV_PAD:CONV_PAD + CHUNK, :] = uc_ref[...]
    win_ref[CONV_PAD + CHUNK:, :] = un_ref[...] * keep_next
    first = CONV_PAD - CONV_WIDTH // 2
    span = CONV_ROWS + 2 * CONV_PAD

    def tile(i, carry):
        r0 = pl.multiple_of(i * CONV_ROWS, CONV_ROWS)
        halves = []
        for c0 in range(0, D_CONV, CONV_LANES):
            win = win_ref[pl.ds(r0, span), c0:c0 + CONV_LANES]
            acc = jnp.zeros((CONV_ROWS, CONV_LANES), F32)
            for phase in range(8):
                taps = [k for k in range(CONV_WIDTH) if (first + k) % 8 == phase]
                if not taps:
                    continue
                shifted = win if phase == 0 else pltpu.roll(win, span - phase, axis=0)
                for k in taps:
                    off = first + k - phase
                    acc = acc + shifted[off:off + CONV_ROWS, :] * w_ref[k:k + 1, c0:c0 + CONV_LANES]
            halves.append(acc)
        y = jnp.concatenate(halves, axis=1) + b_ref[...]
        mu = jnp.mean(y, axis=-1, keepdims=True)
        yc = y - mu
        var = jnp.mean(yc * yc, axis=-1, keepdims=True)
        yn = yc * lax.rsqrt(var + EPS) * g_ref[...] + b2_ref[...]
        o_ref[pl.ds(r0, CONV_ROWS), :] = (yn * _sigmoid(yn)).astype(BF16)
        return carry

    lax.fori_loop(0, CHUNK // CONV_ROWS, tile, 0)


def _conformer_conv(u, has_prev, has_next, conv_w, conv_b, ln_g, ln_b):
    n = u.shape[0]
    n_chunks = n // CHUNK
    per = CHUNK // CONV_PAD
    last = n // CONV_PAD - 1
    full = lambda a: pl.BlockSpec(a.shape, lambda c, hp, hn: (0,) * a.ndim)
    grid_spec = pltpu.PrefetchScalarGridSpec(
        num_scalar_prefetch=2,
        grid=(n_chunks,),
        in_specs=[
            pl.BlockSpec((CONV_PAD, D_CONV), lambda c, hp, hn: (jnp.maximum(c * per - 1, 0), 0)),
            pl.BlockSpec((CHUNK, D_CONV), lambda c, hp, hn: (c, 0)),
            pl.BlockSpec((CONV_PAD, D_CONV), lambda c, hp, hn: (jnp.minimum((c + 1) * per, last), 0)),
            full(conv_w), full(conv_b), full(ln_g), full(ln_b),
        ],
        out_specs=pl.BlockSpec((CHUNK, D_CONV), lambda c, hp, hn: (c, 0)),
        scratch_shapes=[pltpu.VMEM((CHUNK + 2 * CONV_PAD, D_CONV), F32)],
    )
    return pl.pallas_call(
        _conv_body,
        grid_spec=grid_spec,
        out_shape=jax.ShapeDtypeStruct((n, D_CONV), BF16),
        compiler_params=_params("parallel"),
        name="conformer_conv",
    )(has_prev, has_next, u, u, u, conv_w, conv_b, ln_g, ln_b)


def _t5_bucket(rel):
    half = N_BUCKETS // 2
    max_exact = half // 2
    ret = np.where(rel > 0, half, 0)
    n = np.abs(rel)
    large = max_exact + (np.log(np.maximum(n, 1) / max_exact)
                         / np.log(MAX_DISTANCE / max_exact) * (half - max_exact)).astype(np.int32)
    large = np.minimum(large, half - 1)
    return (ret + np.where(n < max_exact, n, large)).astype(np.int32)


def _bias_tables(rel_bias):
    delta = np.arange(QGROUP + KGROUP - 1) - (QGROUP - 1) - ATTN_HALF
    band = np.abs(delta) <= ATTN_HALF
    tabs = []
    for d in ATTN_DILATIONS:
        onehot = np.eye(N_BUCKETS, dtype=np.float32)[_t5_bucket(d * delta)]
        base = jnp.dot(onehot, rel_bias.astype(F32), precision=lax.Precision.HIGHEST)
        base = jnp.where(band[:, None], base, NEG)
        rows = [base[QGROUP - 1 - i:QGROUP - 1 - i + KGROUP] for i in range(QGROUP)]
        b = jnp.transpose(jnp.stack(rows, axis=0), (2, 0, 1))
        tabs.append(b.reshape(N_HEADS // 2, 2 * QGROUP, KGROUP))
    return jnp.stack(tabs, axis=0)


ATTN_UNROLL = 2
MERGE_ROWS = 256


def _attn_body(hp_ref, hn_ref, q_ref, kp_ref, kc_ref, kn_ref, vp_ref, vc_ref, vn_ref, tab_ref, o_ref,
               kw_ref, vw_ref, *part_refs):
    nb = len(ATTN_DILATIONS)
    ob_refs, mb_refs, lb_refs = part_refs[:nb], part_refs[nb:2 * nb], part_refs[2 * nb:]
    c = pl.program_id(0)
    has_prev = hp_ref[c] > 0
    has_next = hn_ref[c] > 0
    kw_ref[0:HALO, :] = kp_ref[...]
    kw_ref[HALO:HALO + CHUNK, :] = kc_ref[...]
    kw_ref[HALO + CHUNK:, :] = kn_ref[...]
    vw_ref[0:HALO, :] = vp_ref[...]
    vw_ref[HALO:HALO + CHUNK, :] = vc_ref[...]
    vw_ref[HALO + CHUNK:, :] = vn_ref[...]

    first_head = lax.broadcasted_iota(I32, (QGROUP, LANES), 1) < HEAD_DIM
    key_pos = lax.broadcasted_iota(I32, (1, KGROUP), 1)

    for bi, d in enumerate(ATTN_DILATIONS):
        stride = None if d == 1 else d

        def group(g, bi=bi, d=d, stride=stride):
            res = g % d
            blk = g // d
            q0 = blk * (QGROUP * d) + res
            k0 = HALO + (blk * QGROUP - ATTN_HALF) * d + res
            q = q_ref[pl.ds(q0, QGROUP, stride=stride), :]
            k = kw_ref[pl.ds(k0, KGROUP, stride=stride), :]
            v = vw_ref[pl.ds(k0, KGROUP, stride=stride), :]
            row = k0 + d * key_pos
            valid = ((row >= HALO) | has_prev) & ((row < HALO + CHUNK) | has_next)
            q2 = jnp.concatenate([jnp.where(first_head, q, 0.0), jnp.where(first_head, 0.0, q)], axis=0)
            s = lax.dot_general(q2.astype(BF16), k.astype(BF16), (((1,), (1,)), ((), ())),
                                preferred_element_type=F32)
            s = jnp.where(valid, s + tab_ref[bi, 0], NEG)
            m = jnp.max(s, axis=-1, keepdims=True)
            p = jnp.exp(s - m)
            l = jnp.sum(p, axis=-1, keepdims=True)
            o = jnp.dot(p.astype(BF16), v.astype(BF16), preferred_element_type=F32)
            rows = pl.ds(q0, QGROUP, stride=stride)
            ob_refs[bi][rows, :] = jnp.where(first_head, o[:QGROUP], o[QGROUP:])
            mb_refs[bi][rows, :] = jnp.where(first_head, m[:QGROUP], m[QGROUP:])
            lb_refs[bi][rows, :] = jnp.where(first_head, l[:QGROUP], l[QGROUP:])

        def trip(t, carry, group=group):
            for u in range(ATTN_UNROLL):
                group(t * ATTN_UNROLL + u)
            return carry

        lax.fori_loop(0, CHUNK // QGROUP // ATTN_UNROLL, trip, 0)

    def merge(t, carry):
        rows = pl.ds(pl.multiple_of(t * MERGE_ROWS, MERGE_ROWS), MERGE_ROWS)
        ms = [r[rows, :] for r in mb_refs]
        m_tot = functools.reduce(jnp.maximum, ms)
        ws = [jnp.exp(m - m_tot) for m in ms]
        num = sum(w * r[rows, :] for w, r in zip(ws, ob_refs))
        den = sum(w * r[rows, :] for w, r in zip(ws, lb_refs))
        o_ref[rows, :] = (num / den).astype(BF16)
        return carry

    lax.fori_loop(0, CHUNK // MERGE_ROWS, merge, 0)


def _dilated_attention(q, k, v, has_prev, has_next, tables):
    n = q.shape[0]
    n_chunks = n // CHUNK
    per = CHUNK // HALO
    last = n // HALO - 1
    cur = pl.BlockSpec((CHUNK, LANES), lambda c, h, hp, hn: (c, h))
    prev = pl.BlockSpec((HALO, LANES), lambda c, h, hp, hn: (jnp.maximum(c * per - 1, 0), h))
    nxt = pl.BlockSpec((HALO, LANES), lambda c, h, hp, hn: (jnp.minimum((c + 1) * per, last), h))
    grid_spec = pltpu.PrefetchScalarGridSpec(
        num_scalar_prefetch=2,
        grid=(n_chunks, D_ATTN // LANES),
        in_specs=[cur, prev, cur, nxt, prev, cur, nxt,
                  pl.BlockSpec((len(ATTN_DILATIONS), 1, 2 * QGROUP, KGROUP), lambda c, h, hp, hn: (0, h, 0, 0))],
        out_specs=cur,
        scratch_shapes=([pltpu.VMEM((CHUNK + 2 * HALO, LANES), F32)] * 2
                        + [pltpu.VMEM((CHUNK, LANES), F32)] * (3 * len(ATTN_DILATIONS))),
    )
    return pl.pallas_call(
        _attn_body,
        grid_spec=grid_spec,
        out_shape=jax.ShapeDtypeStruct((n, D_ATTN), BF16),
        compiler_params=_params("parallel", "parallel"),
        name="dilated_attention",
    )(has_prev, has_next, q, k, k, k, v, v, v, tables)


def _mixer_out_body(cv_ref, at_ref, x_ref, w_ref, g_ref, wr_ref, xo_ref, h_ref, aff_ref):
    y = (jnp.dot(cv_ref[...], w_ref[0:D_CONV, :], preferred_element_type=F32)
         + jnp.dot(at_ref[...], w_ref[D_CONV:, :], preferred_element_type=F32))
    x = _load_tokens(x_ref, ROW_TILE) + y
    _store_tokens(xo_ref, x)
    h = x * lax.rsqrt(jnp.mean(x * x, axis=-1, keepdims=True) + EPS) * g_ref[...]
    _store_tokens(h_ref, h)
    logits = lax.dot_general(wr_ref[...], h, (((1,), (1,)), ((), ())),
                             precision=lax.Precision.HIGHEST, preferred_element_type=F32)
    e = jnp.exp(logits - jnp.max(logits, axis=0, keepdims=True))
    aff_ref[...] = e / jnp.sum(e, axis=0, keepdims=True)


def _mixer_out(conv_out, attn_out, xt, w_out_bf16, g, w_router_t):
    n = xt.shape[0] // SLABS
    row = lambda w: pl.BlockSpec((ROW_TILE, w), lambda i: (i, 0))
    full = lambda a: pl.BlockSpec(a.shape, lambda i: (0,) * a.ndim)
    tiled = jax.ShapeDtypeStruct(xt.shape, F32)
    return pl.pallas_call(
        _mixer_out_body,
        grid=(n // ROW_TILE,),
        in_specs=[row(D_CONV), row(D_ATTN), _tiled_spec(ROW_TILE), full(w_out_bf16), full(g), full(w_router_t)],
        out_specs=[_tiled_spec(ROW_TILE), _tiled_spec(ROW_TILE),
                   pl.BlockSpec((N_EXPERTS, ROW_TILE), lambda i: (0, i))],
        out_shape=[tiled, tiled, jax.ShapeDtypeStruct((N_EXPERTS, n), F32)],
        input_output_aliases={2: 0},
        compiler_params=_params("parallel"),
        name="mixer_out_router",
    )(conv_out, attn_out, xt, w_out_bf16, g, w_router_t)


def _select_body(aff_ref, idx_ref, gate_ref, *, cap, tok0):
    a = aff_ref[0]
    segs = a.shape[0]
    bits = pltpu.bitcast(a, I32)

    def count(mask):
        return jnp.sum(jnp.sum(mask.astype(F32), axis=0, keepdims=True), axis=1, keepdims=True)

    thr = jnp.zeros((1, 1), I32)
    for bit in range(30, -1, -1):
        cand = thr | (1 << bit)
        thr = jnp.where(count(bits >= cand) >= cap, cand, thr)
    above = bits > thr
    equal = bits == thr
    need = cap - count(above)

    lane_r = lax.broadcasted_iota(I32, (LANES, LANES), 0)
    lane_c = lax.broadcasted_iota(I32, (LANES, LANES), 1)
    upper = (lane_r <= lane_c).astype(BF16)
    seg_r = lax.broadcasted_iota(I32, (segs, segs), 0)
    seg_c = lax.broadcasted_iota(I32, (segs, segs), 1)
    before = (seg_c < seg_r).astype(BF16)

    def prefix(mask):
        incl = jnp.dot(mask.astype(BF16), upper, preferred_element_type=F32)
        tot = jnp.broadcast_to(incl[:, LANES - 1:LANES], (segs, LANES))
        pre = jnp.dot(before, tot.astype(BF16), preferred_element_type=F32)
        return incl, pre, tot

    eq_f = equal.astype(F32)
    incl_e, pre_e, _ = prefix(eq_f)
    chosen = above | (equal & (pre_e + incl_e - eq_f < need))
    incl, pre, tot = prefix(chosen.astype(F32))
    seg_lo = pre[:, 0:1]
    seg_hi = seg_lo + tot[:, 0:1]
    incl_t = jnp.transpose(incl).astype(BF16)
    aff_t = jnp.transpose(a)
    lane_id = lax.broadcasted_iota(I32, (LANES, SLOT_BLOCK), 0).astype(F32)

    for jb in range(cap // SLOT_BLOCK):
        slot = (jb * SLOT_BLOCK + lax.broadcasted_iota(I32, (1, SLOT_BLOCK), 1)).astype(F32)
        seg_of = jnp.sum((seg_hi <= slot).astype(F32), axis=0, keepdims=True)
        onehot = ((seg_lo <= slot) & (slot < seg_hi)).astype(F32)
        rank = slot - jnp.sum(onehot * seg_lo, axis=0, keepdims=True)
        counts = jnp.dot(incl_t, onehot.astype(BF16), preferred_element_type=F32)
        lane_of = jnp.sum((counts <= rank).astype(F32), axis=0, keepdims=True)
        token = (seg_of * LANES + lane_of).astype(I32) + tok0
        idx_ref[0, :, jb * SLOT_BLOCK:(jb + 1) * SLOT_BLOCK] = token * SLABS
        vals = jnp.dot(aff_t, onehot, precision=lax.Precision.HIGHEST, preferred_element_type=F32)
        gate_ref[0, :, jb * SLOT_BLOCK:(jb + 1) * SLOT_BLOCK] = jnp.sum(
            jnp.where(lane_id == lane_of, vals, 0.0), axis=0, keepdims=True)


def _select(aff_group, cap, tok0):
    n = aff_group.shape[1]
    aff3 = aff_group.reshape(N_EXPERTS, n // LANES, LANES)
    spec = pl.BlockSpec((1, 1, cap), lambda e: (e, 0, 0))
    return pl.pallas_call(
        functools.partial(_select_body, cap=cap, tok0=tok0),
        grid=(N_EXPERTS,),
        in_specs=[pl.BlockSpec((1, n // LANES, LANES), lambda e: (e, 0, 0))],
        out_specs=[spec, spec],
        out_shape=[jax.ShapeDtypeStruct((N_EXPERTS, 1, cap), I32), jax.ShapeDtypeStruct((N_EXPERTS, 1, cap), F32)],
        compiler_params=_params("parallel"),
        name="expert_select",
    )(aff3)


def _ffn_body(idx_ref, h_hbm, gate_ref, wg_ref, wu_ref, wd_ref, x_in_hbm, x_hbm,
              hbuf_ref, xbuf_ref, xb_ref, acc_ref, sem_h, sem_x, sem_s, *, rows, n_tiles, steps):
    del x_in_hbm
    g = pl.program_id(0)
    f = pl.program_id(1)
    cur = g % 2
    oth = 1 - cur
    n_early = steps - 3
    per_early = rows // n_early
    per_late = rows // 2
    base = g * rows
    base_next = jnp.minimum(g + 1, n_tiles - 1) * rows
    base_prev = jnp.maximum(g - 1, 0) * rows

    def hbm_tok(hbm, tile_base, j):
        return hbm.at[pl.ds(pl.multiple_of(idx_ref[tile_base + j], SLABS), SLABS), :]

    def buf_tok(buf, slot, j):
        return buf.at[slot, pl.ds(pl.multiple_of(j * SLABS, SLABS), SLABS), :]

    def h_gather(tile_base, j, slot):
        return pltpu.make_async_copy(hbm_tok(h_hbm, tile_base, j), buf_tok(hbuf_ref, slot, j), sem_h.at[slot])

    def x_gather(tile_base, j, slot):
        return pltpu.make_async_copy(hbm_tok(x_hbm, tile_base, j), buf_tok(xbuf_ref, slot, j), sem_x.at[slot])

    def x_scatter(tile_base, j, slot):
        return pltpu.make_async_copy(buf_tok(xbuf_ref, slot, j), hbm_tok(x_hbm, tile_base, j), sem_s.at[slot])

    def wait_h(slot):
        pltpu.make_async_copy(h_hbm.at[pl.ds(0, SLABS * rows), :], hbuf_ref.at[slot], sem_h.at[slot]).wait()

    def wait_x(slot):
        pltpu.make_async_copy(x_hbm.at[pl.ds(0, SLABS * rows), :], xbuf_ref.at[slot], sem_x.at[slot]).wait()

    def wait_s(slot):
        pltpu.make_async_copy(xbuf_ref.at[slot], x_hbm.at[pl.ds(0, SLABS * rows), :], sem_s.at[slot]).wait()

    def stage_h(slot):
        xb_ref[...] = _load_tokens(hbuf_ref.at[slot], rows).astype(BF16)

    def matmuls():
        xb = xb_ref[...]
        hg = jnp.dot(xb, wg_ref[0].astype(BF16), preferred_element_type=F32)
        hu = jnp.dot(xb, wu_ref[0].astype(BF16), preferred_element_type=F32)
        hid = (hg * _sigmoid(hg) * hu).astype(BF16)
        acc_ref[...] += jnp.dot(hid, wd_ref[0].astype(BF16), preferred_element_type=F32)

    @pl.when((g == 0) & (f == 0))
    def _():
        def issue(j, carry):
            h_gather(0, j, 0).start()
            x_gather(0, j, 1).start()
            return carry
        lax.fori_loop(0, rows, issue, 0)
        wait_h(0)
        wait_x(1)
        stage_h(0)
        acc_ref[...] = jnp.zeros_like(acc_ref)

    @pl.when(f < n_early)
    def _():
        j0 = f * per_early
        for i in range(per_early):
            h_gather(base_next, j0 + i, oth).start()
            x_scatter(base_prev, j0 + i, oth).start()
        matmuls()

    @pl.when(f == n_early)
    def _():
        wait_s(oth)

    @pl.when((f >= n_early) & (f < steps - 1))
    def _():
        j0 = (f - n_early) * per_late
        for i in range(per_late):
            x_gather(base, j0 + i, cur).start()
        matmuls()

    @pl.when(f == steps - 1)
    def _():
        matmuls()
        wait_x(cur)
        xrows = xbuf_ref.at[cur]
        gate = gate_ref[0]
        for s in range(SLABS):
            slab = pl.ds(s, rows, stride=SLABS)
            xrows[slab, :] = xrows[slab, :] + acc_ref[:, s * LANES:(s + 1) * LANES] * gate
        wait_h(oth)
        stage_h(oth)
        acc_ref[...] = jnp.zeros_like(acc_ref)

    @pl.when((g == n_tiles - 1) & (f == steps - 1))
    def _():
        def issue(j, carry):
            x_scatter(base, j, cur).start()
            return carry
        lax.fori_loop(0, rows, issue, 0)
        wait_s(cur)


def _expert_ffn(idx_flat, h, gates, w_gate, w_up, w_down, x):
    n_slots = gates.shape[1]
    d_exp = w_gate.shape[2]
    tiles = n_slots // FFN_ROWS
    steps = d_exp // FFN_COLS
    assert steps >= 4 and FFN_ROWS % (steps - 3) == 0 and FFN_ROWS % 2 == 0
    any_spec = pl.BlockSpec(memory_space=pl.ANY)
    grid_spec = pltpu.PrefetchScalarGridSpec(
        num_scalar_prefetch=1,
        grid=(N_EXPERTS * tiles, steps),
        in_specs=[
            any_spec,
            pl.BlockSpec((1, FFN_ROWS, 1), lambda g, f, idx: (g // tiles, g % tiles, 0)),
            pl.BlockSpec((1, D_MODEL, FFN_COLS), lambda g, f, idx: (g // tiles, 0, f)),
            pl.BlockSpec((1, D_MODEL, FFN_COLS), lambda g, f, idx: (g // tiles, 0, f)),
            pl.BlockSpec((1, FFN_COLS, D_MODEL), lambda g, f, idx: (g // tiles, f, 0)),
            any_spec,
        ],
        out_specs=any_spec,
        scratch_shapes=[pltpu.VMEM((2, SLABS * FFN_ROWS, LANES), F32), pltpu.VMEM((2, SLABS * FFN_ROWS, LANES), F32),
                        pltpu.VMEM((FFN_ROWS, D_MODEL), BF16), pltpu.VMEM((FFN_ROWS, D_MODEL), F32),
                        pltpu.SemaphoreType.DMA((2,)), pltpu.SemaphoreType.DMA((2,)),
                        pltpu.SemaphoreType.DMA((2,))],
    )
    return pl.pallas_call(
        functools.partial(_ffn_body, rows=FFN_ROWS, n_tiles=N_EXPERTS * tiles, steps=steps),
        grid_spec=grid_spec,
        out_shape=jax.ShapeDtypeStruct(x.shape, F32),
        input_output_aliases={6: 0},
        compiler_params=_params("arbitrary", "arbitrary"),
        name="expert_ffn",
    )(idx_flat, h, gates, w_gate, w_up, w_down, x)


def _final_norm_body(x_ref, g_ref, o_ref):
    x = _load_tokens(x_ref, ROW_TILE)
    o_ref[...] = x * lax.rsqrt(jnp.mean(x * x, axis=-1, keepdims=True) + EPS) * g_ref[...]


def _final_norm(xt, g, tok0, n):
    return pl.pallas_call(
        _final_norm_body,
        grid=(n // ROW_TILE,),
        in_specs=[_tiled_spec(ROW_TILE, tok0 // ROW_TILE), pl.BlockSpec(g.shape, lambda i: (0, 0))],
        out_specs=pl.BlockSpec((ROW_TILE, D_MODEL), lambda i: (i, 0)),
        out_shape=jax.ShapeDtypeStruct((n, D_MODEL), F32),
        compiler_params=_params("parallel"),
        name="final_norm",
    )(xt, g)


def _chunk_neighbours(seq_lens):
    has_prev, has_next = [], []
    for length in seq_lens:
        assert length % CHUNK == 0
        k = length // CHUNK
        has_prev += [int(i > 0) for i in range(k)]
        has_next += [int(i < k - 1) for i in range(k)]
    return jnp.asarray(has_prev, I32), jnp.asarray(has_next, I32)


def _trunk(x, groups, norm_mix_g, w_in, conv_w, conv_b, conv_ln_g, conv_ln_b, w_out, rel_bias,
           norm_ffn_g, w_router, w_gate, w_up, w_down, norm_final_g):
    seq_lens = [s for b, s in groups for _ in range(b)]
    has_prev, has_next = _chunk_neighbours(seq_lens)
    tables = _bias_tables(rel_bias)
    depth = w_in.shape[0]
    row2 = lambda a: a.reshape(1, -1)
    for l in range(depth):
        u, q, k, v = _mixer_in(x, row2(norm_mix_g[l]), w_in[l].astype(BF16))
        conv_out = _conformer_conv(u, has_prev, has_next, conv_w[l], row2(conv_b[l]),
                                   row2(conv_ln_g[l]), row2(conv_ln_b[l]))
        attn_out = _dilated_attention(q, k, v, has_prev, has_next, tables)
        x, h, aff = _mixer_out(conv_out, attn_out, x, w_out[l].astype(BF16), row2(norm_ffn_g[l]),
                               jnp.transpose(w_router[l]))
        idx, gates, tok0 = [], [], 0
        for b, s in groups:
            n = b * s
            cap = CAPACITY_FACTOR * n // N_EXPERTS
            i, g = _select(aff[:, tok0:tok0 + n], cap, tok0)
            idx.append(i[:, 0, :])
            gates.append(g[:, 0, :])
            tok0 += n
        idx = jnp.concatenate(idx, axis=1)
        gates = jnp.concatenate(gates, axis=1)[:, :, None]
        x = _expert_ffn(idx.reshape(-1), h, gates, w_gate[l], w_up[l], w_down[l], x)
    outs, tok0 = [], 0
    for b, s in groups:
        outs.append(_final_norm(x, row2(norm_final_g), tok0, b * s).reshape(b, s, D_MODEL))
        tok0 += b * s
    return outs


def kernel(x_prompt, x_sample, norm_mix_g, w_in, conv_w, conv_b, conv_ln_g, conv_ln_b, w_out, rel_bias,
           norm_ffn_g, w_router, w_gate, w_up, w_down, norm_final_g):
    groups = [x_prompt.shape[:2], x_sample.shape[:2]]
    x = _row_tiled_concat(x_prompt.reshape(-1, D_MODEL), x_sample.reshape(-1, D_MODEL))
    y_prompt, y_sample = _trunk(x, groups, norm_mix_g, w_in, conv_w, conv_b, conv_ln_g, conv_ln_b, w_out,
                                rel_bias, norm_ffn_g, w_router, w_gate, w_up, w_down, norm_final_g)
    return y_prompt, y_sample
```

```python
import functools

import jax
import jax.numpy as jnp
import numpy as np
from jax import lax
from jax.experimental import pallas as pl
from jax.experimental.pallas import tpu as pltpu

F32 = jnp.float32
BF16 = jnp.bfloat16
I32 = jnp.int32

D_MODEL = 1024
D_CONV = 512
N_HEADS = 8
HEAD_DIM = 64
D_ATTN = N_HEADS * HEAD_DIM
CONV_WIDTH = 31
ATTN_DILATIONS = (1, 4, 16)
ATTN_HALF = 64
N_BUCKETS = 32
MAX_DISTANCE = 1024
N_EXPERTS = 16
CAPACITY_FACTOR = 2
EPS = 1e-6
NEG = -1e30

LANES = 128
QGROUP = 128
KGROUP = QGROUP + 2 * ATTN_HALF
CHUNK = QGROUP * max(ATTN_DILATIONS)
HALO = ATTN_HALF * max(ATTN_DILATIONS)
CONV_PAD = 16
ROW_TILE = 512
FFN_ROWS = 1024
FFN_COLS = 256
SLOT_BLOCK = 512
VMEM_LIMIT = 56 * 1024 * 1024


def _params(*sem):
    return pltpu.CompilerParams(dimension_semantics=sem, vmem_limit_bytes=VMEM_LIMIT)


def _sigmoid(x):
    return 1.0 / (1.0 + jnp.exp(-x))


SLABS = D_MODEL // LANES


def _load_tokens(ref, n):
    return jnp.concatenate([ref[pl.ds(s, n, stride=SLABS), :] for s in range(SLABS)], axis=1)


def _store_tokens(ref, val):
    n = val.shape[0]
    for s in range(SLABS):
        ref[pl.ds(s, n, stride=SLABS), :] = val[:, s * LANES:(s + 1) * LANES]


def _tiled_spec(rows, first=0):
    return pl.BlockSpec((SLABS * rows, LANES), lambda i, *_: (i + first, 0))


def _row_tiled_body(xa_ref, xb_ref, o_ref, *, tiles_a):
    i = pl.program_id(0)

    @pl.when(i < tiles_a)
    def _():
        _store_tokens(o_ref, xa_ref[...])

    @pl.when(i >= tiles_a)
    def _():
        _store_tokens(o_ref, xb_ref[...])


def _row_tiled_concat(xa, xb):
    tiles_a = xa.shape[0] // ROW_TILE
    tiles_b = xb.shape[0] // ROW_TILE
    n = xa.shape[0] + xb.shape[0]
    return pl.pallas_call(
        functools.partial(_row_tiled_body, tiles_a=tiles_a),
        grid=(tiles_a + tiles_b,),
        in_specs=[pl.BlockSpec((ROW_TILE, D_MODEL), lambda i: (jnp.minimum(i, tiles_a - 1), 0)),
                  pl.BlockSpec((ROW_TILE, D_MODEL), lambda i: (jnp.maximum(i - tiles_a, 0), 0))],
        out_specs=_tiled_spec(ROW_TILE),
        out_shape=jax.ShapeDtypeStruct((SLABS * n, LANES), F32),
        compiler_params=_params("parallel"),
        name="row_tiled_concat",
    )(xa, xb)


def _mixer_in_body(x_ref, g_ref, w_ref, u_ref, q_ref, k_ref, v_ref):
    x = _load_tokens(x_ref, ROW_TILE)
    xn = (x * lax.rsqrt(jnp.mean(x * x, axis=-1, keepdims=True) + EPS) * g_ref[...]).astype(BF16)

    def proj(j, width):
        return jnp.dot(xn, w_ref[:, j:j + width], preferred_element_type=F32)

    val = proj(0, D_CONV)
    gate = proj(D_CONV, D_CONV)
    u_ref[...] = val * _sigmoid(gate)
    q_ref[...] = proj(2 * D_CONV, D_ATTN) * (HEAD_DIM ** -0.5)
    k_ref[...] = proj(2 * D_CONV + D_ATTN, D_ATTN)
    v_ref[...] = proj(2 * D_CONV + 2 * D_ATTN, D_ATTN)


def _mixer_in(xt, g, w_in_bf16):
    n = xt.shape[0] // SLABS
    row = lambda w: pl.BlockSpec((ROW_TILE, w), lambda i: (i, 0))
    full = lambda a: pl.BlockSpec(a.shape, lambda i: (0,) * a.ndim)
    out = lambda w: jax.ShapeDtypeStruct((n, w), F32)
    return pl.pallas_call(
        _mixer_in_body,
        grid=(n // ROW_TILE,),
        in_specs=[_tiled_spec(ROW_TILE), full(g), full(w_in_bf16)],
        out_specs=[row(D_CONV), row(D_ATTN), row(D_ATTN), row(D_ATTN)],
        out_shape=[out(D_CONV), out(D_ATTN), out(D_ATTN), out(D_ATTN)],
        compiler_params=_params("parallel"),
        name="mixer_in",
    )(xt, g, w_in_bf16)


CONV_ROWS = 64
CONV_LANES = 256


def _conv_body(hp_ref, hn_ref, up_ref, uc_ref, un_ref, w_ref, b_ref, g_ref, b2_ref, o_ref, win_ref):
    c = pl.program_id(0)
    keep_prev = (hp_ref[c] > 0).astype(F32)
    keep_next = (hn_ref[c] > 0).astype(F32)
    win_ref[0:CONV_PAD, :] = up_ref[...] * keep_prev
    win_ref[CONV_PAD:CONV_PAD + CHUNK, :] = uc_ref[...]
    win_ref[CONV_PAD + CHUNK:, :] = un_ref[...] * keep_next
    first = CONV_PAD - CONV_WIDTH // 2
    span = CONV_ROWS + 2 * CONV_PAD

    def tile(i, carry):
        r0 = pl.multiple_of(i * CONV_ROWS, CONV_ROWS)
        halves = []
        for c0 in range(0, D_CONV, CONV_LANES):
            win = win_ref[pl.ds(r0, span), c0:c0 + CONV_LANES]
            acc = jnp.zeros((CONV_ROWS, CONV_LANES), F32)
            for phase in range(8):
                taps = [k for k in range(CONV_WIDTH) if (first + k) % 8 == phase]
                if not taps:
                    continue
                shifted = win if phase == 0 else pltpu.roll(win, span - phase, axis=0)
                for k in taps:
                    off = first + k - phase
                    acc = acc + shifted[off:off + CONV_ROWS, :] * w_ref[k:k + 1, c0:c0 + CONV_LANES]
            halves.append(acc)
        y = jnp.concatenate(halves, axis=1) + b_ref[...]
        mu = jnp.mean(y, axis=-1, keepdims=True)
        yc = y - mu
        var = jnp.mean(yc * yc, axis=-1, keepdims=True)
        yn = yc * lax.rsqrt(var + EPS) * g_ref[...] + b2_ref[...]
        o_ref[pl.ds(r0, CONV_ROWS), :] = (yn * _sigmoid(yn)).astype(BF16)
        return carry

    lax.fori_loop(0, CHUNK // CONV_ROWS, tile, 0)


def _conformer_conv(u, has_prev, has_next, conv_w, conv_b, ln_g, ln_b):
    n = u.shape[0]
    n_chunks = n // CHUNK
    per = CHUNK // CONV_PAD
    last = n // CONV_PAD - 1
    full = lambda a: pl.BlockSpec(a.shape, lambda c, hp, hn: (0,) * a.ndim)
    grid_spec = pltpu.PrefetchScalarGridSpec(
        num_scalar_prefetch=2,
        grid=(n_chunks,),
        in_specs=[
            pl.BlockSpec((CONV_PAD, D_CONV), lambda c, hp, hn: (jnp.maximum(c * per - 1, 0), 0)),
            pl.BlockSpec((CHUNK, D_CONV), lambda c, hp, hn: (c, 0)),
            pl.BlockSpec((CONV_PAD, D_CONV), lambda c, hp, hn: (jnp.minimum((c + 1) * per, last), 0)),
            full(conv_w), full(conv_b), full(ln_g), full(ln_b),
        ],
        out_specs=pl.BlockSpec((CHUNK, D_CONV), lambda c, hp, hn: (c, 0)),
        scratch_shapes=[pltpu.VMEM((CHUNK + 2 * CONV_PAD, D_CONV), F32)],
    )
    return pl.pallas_call(
        _conv_body,
        grid_spec=grid_spec,
        out_shape=jax.ShapeDtypeStruct((n, D_CONV), BF16),
        compiler_params=_params("parallel"),
        name="conformer_conv",
    )(has_prev, has_next, u, u, u, conv_w, conv_b, ln_g, ln_b)


def _t5_bucket(rel):
    half = N_BUCKETS // 2
    max_exact = half // 2
    ret = np.where(rel > 0, half, 0)
    n = np.abs(rel)
    large = max_exact + (np.log(np.maximum(n, 1) / max_exact)
                         / np.log(MAX_DISTANCE / max_exact) * (half - max_exact)).astype(np.int32)
    large = np.minimum(large, half - 1)
    return (ret + np.where(n < max_exact, n, large)).astype(np.int32)


def _bias_tables(rel_bias):
    delta = np.arange(QGROUP + KGROUP - 1) - (QGROUP - 1) - ATTN_HALF
    band = np.abs(delta) <= ATTN_HALF
    tabs = []
    for d in ATTN_DILATIONS:
        onehot = np.eye(N_BUCKETS, dtype=np.float32)[_t5_bucket(d * delta)]
        base = jnp.dot(onehot, rel_bias.astype(F32), precision=lax.Precision.HIGHEST)
        base = jnp.where(band[:, None], base, NEG)
        rows = [base[QGROUP - 1 - i:QGROUP - 1 - i + KGROUP] for i in range(QGROUP)]
        b = jnp.transpose(jnp.stack(rows, axis=0), (2, 0, 1))
        tabs.append(b.reshape(N_HEADS // 2, 2 * QGROUP, KGROUP))
    return jnp.stack(tabs, axis=0)


MERGE_ROWS = 256


def _attn_body(hp_ref, hn_ref, q_ref, kp_ref, kc_ref, kn_ref, vp_ref, vc_ref, vn_ref, tab_ref, o_ref,
               kw_ref, vw_ref, *part_refs):
    nb = len(ATTN_DILATIONS)
    ob_refs, mb_refs, lb_refs = part_refs[:nb], part_refs[nb:2 * nb], part_refs[2 * nb:]
    c = pl.program_id(0)
    has_prev = hp_ref[c] > 0
    has_next = hn_ref[c] > 0
    kw_ref[0:HALO, :] = kp_ref[...]
    kw_ref[HALO:HALO + CHUNK, :] = kc_ref[...]
    kw_ref[HALO + CHUNK:, :] = kn_ref[...]
    vw_ref[0:HALO, :] = vp_ref[...]
    vw_ref[HALO:HALO + CHUNK, :] = vc_ref[...]
    vw_ref[HALO + CHUNK:, :] = vn_ref[...]

    first_head = lax.broadcasted_iota(I32, (QGROUP, LANES), 1) < HEAD_DIM
    key_pos = lax.broadcasted_iota(I32, (1, KGROUP), 1)

    for bi, d in enumerate(ATTN_DILATIONS):
        stride = None if d == 1 else d
        for g in range(CHUNK // QGROUP):
            res, blk = g % d, g // d
            q0 = blk * (QGROUP * d) + res
            k0 = HALO + (blk * QGROUP - ATTN_HALF) * d + res
            k_last = k0 + d * (KGROUP - 1)
            q = q_ref[pl.ds(q0, QGROUP, stride=stride), :]
            k = kw_ref[pl.ds(k0, KGROUP, stride=stride), :]
            v = vw_ref[pl.ds(k0, KGROUP, stride=stride), :]
            q2 = jnp.concatenate([jnp.where(first_head, q, 0.0), jnp.where(first_head, 0.0, q)], axis=0)
            s = lax.dot_general(q2.astype(BF16), k.astype(BF16), (((1,), (1,)), ((), ())),
                                preferred_element_type=F32)
            s = s + tab_ref[bi, 0]
            row = k0 + d * key_pos
            if k0 < HALO:
                s = jnp.where((row >= HALO) | has_prev, s, NEG)
            if k_last >= HALO + CHUNK:
                s = jnp.where((row < HALO + CHUNK) | has_next, s, NEG)
            m = jnp.max(s, axis=-1, keepdims=True)
            p = jnp.exp(s - m)
            l = jnp.sum(p, axis=-1, keepdims=True)
            o = jnp.dot(p.astype(BF16), v.astype(BF16), preferred_element_type=F32)
            rows = pl.ds(q0, QGROUP, stride=stride)
            ob_refs[bi][rows, :] = jnp.where(first_head, o[:QGROUP], o[QGROUP:])
            mb_refs[bi][rows, :] = jnp.where(first_head, m[:QGROUP], m[QGROUP:])
            lb_refs[bi][rows, :] = jnp.where(first_head, l[:QGROUP], l[QGROUP:])

    def merge(t, carry):
        rows = pl.ds(pl.multiple_of(t * MERGE_ROWS, MERGE_ROWS), MERGE_ROWS)
        ms = [r[rows, :] for r in mb_refs]
        m_tot = functools.reduce(jnp.maximum, ms)
        ws = [jnp.exp(m - m_tot) for m in ms]
        num = sum(w * r[rows, :] for w, r in zip(ws, ob_refs))
        den = sum(w * r[rows, :] for w, r in zip(ws, lb_refs))
        o_ref[rows, :] = (num / den).astype(BF16)
        return carry

    lax.fori_loop(0, CHUNK // MERGE_ROWS, merge, 0)


def _dilated_attention(q, k, v, has_prev, has_next, tables):
    n = q.shape[0]
    n_chunks = n // CHUNK
    per = CHUNK // HALO
    last = n // HALO - 1
    cur = pl.BlockSpec((CHUNK, LANES), lambda c, h, hp, hn: (c, h))
    prev = pl.BlockSpec((HALO, LANES), lambda c, h, hp, hn: (jnp.maximum(c * per - 1, 0), h))
    nxt = pl.BlockSpec((HALO, LANES), lambda c, h, hp, hn: (jnp.minimum((c + 1) * per, last), h))
    grid_spec = pltpu.PrefetchScalarGridSpec(
        num_scalar_prefetch=2,
        grid=(n_chunks, D_ATTN // LANES),
        in_specs=[cur, prev, cur, nxt, prev, cur, nxt,
                  pl.BlockSpec((len(ATTN_DILATIONS), 1, 2 * QGROUP, KGROUP), lambda c, h, hp, hn: (0, h, 0, 0))],
        out_specs=cur,
        scratch_shapes=([pltpu.VMEM((CHUNK + 2 * HALO, LANES), F32)] * 2
                        + [pltpu.VMEM((CHUNK, LANES), F32)] * (3 * len(ATTN_DILATIONS))),
    )
    return pl.pallas_call(
        _attn_body,
        grid_spec=grid_spec,
        out_shape=jax.ShapeDtypeStruct((n, D_ATTN), BF16),
        compiler_params=_params("parallel", "parallel"),
        name="dilated_attention",
    )(has_prev, has_next, q, k, k, k, v, v, v, tables)


def _mixer_out_body(cv_ref, at_ref, x_ref, w_ref, g_ref, wr_ref, xo_ref, h_ref, aff_ref):
    y = (jnp.dot(cv_ref[...], w_ref[0:D_CONV, :], preferred_element_type=F32)
         + jnp.dot(at_ref[...], w_ref[D_CONV:, :], preferred_element_type=F32))
    x = _load_tokens(x_ref, ROW_TILE) + y
    _store_tokens(xo_ref, x)
    h = x * lax.rsqrt(jnp.mean(x * x, axis=-1, keepdims=True) + EPS) * g_ref[...]
    _store_tokens(h_ref, h)
    h_hi = h.astype(BF16)
    h_lo = (h - h_hi.astype(F32)).astype(BF16)
    both = jnp.dot(h_hi, wr_ref[...], preferred_element_type=F32)
    low = jnp.dot(h_lo, wr_ref[:, :LANES], preferred_element_type=F32)
    logits = both[:, :LANES] + both[:, LANES:] + low
    logits = jnp.transpose(logits)[:N_EXPERTS, :]
    e = jnp.exp(logits - jnp.max(logits, axis=0, keepdims=True))
    aff_ref[...] = e / jnp.sum(e, axis=0, keepdims=True)


def _mixer_out(conv_out, attn_out, xt, w_out_bf16, g, w_router_pad):
    n = xt.shape[0] // SLABS
    row = lambda w: pl.BlockSpec((ROW_TILE, w), lambda i: (i, 0))
    full = lambda a: pl.BlockSpec(a.shape, lambda i: (0,) * a.ndim)
    tiled = jax.ShapeDtypeStruct(xt.shape, F32)
    return pl.pallas_call(
        _mixer_out_body,
        grid=(n // ROW_TILE,),
        in_specs=[row(D_CONV), row(D_ATTN), _tiled_spec(ROW_TILE), full(w_out_bf16), full(g), full(w_router_pad)],
        out_specs=[_tiled_spec(ROW_TILE), _tiled_spec(ROW_TILE),
                   pl.BlockSpec((N_EXPERTS, ROW_TILE), lambda i: (0, i))],
        out_shape=[tiled, tiled, jax.ShapeDtypeStruct((N_EXPERTS, n), F32)],
        input_output_aliases={2: 0},
        compiler_params=_params("parallel"),
        name="mixer_out_router",
    )(conv_out, attn_out, xt, w_out_bf16, g, w_router_pad)


def _count(mask):
    return jnp.sum(jnp.sum(mask.astype(F32), axis=0, keepdims=True), axis=1, keepdims=True)


def _threshold_body(aff_ref, thr_ref, *, cap):
    thr = [jnp.zeros((1, 1), I32) for _ in range(N_EXPERTS)]
    for bit in range(29, -1, -1):
        for e in range(N_EXPERTS):
            bits = pltpu.bitcast(aff_ref[e], I32)
            cand = thr[e] | (1 << bit)
            thr[e] = jnp.where(_count(bits >= cand) >= cap, cand, thr[e])
    for e in range(N_EXPERTS):
        thr_ref[e] = jnp.broadcast_to(thr[e], (1, LANES))


def _select_body(aff_ref, thr_ref, idx_ref, gate_ref, *, cap, tok0):
    a = aff_ref[0]
    segs = a.shape[0]
    bits = pltpu.bitcast(a, I32)
    count = _count
    thr = thr_ref[0][:, 0:1]
    above = bits > thr
    equal = bits == thr
    need = cap - count(above)

    lane_r = lax.broadcasted_iota(I32, (LANES, LANES), 0)
    lane_c = lax.broadcasted_iota(I32, (LANES, LANES), 1)
    upper = (lane_r <= lane_c).astype(BF16)
    seg_r = lax.broadcasted_iota(I32, (segs, segs), 0)
    seg_c = lax.broadcasted_iota(I32, (segs, segs), 1)
    before = (seg_c < seg_r).astype(BF16)

    def prefix(mask):
        incl = jnp.dot(mask.astype(BF16), upper, preferred_element_type=F32)
        tot = jnp.broadcast_to(incl[:, LANES - 1:LANES], (segs, LANES))
        pre = jnp.dot(before, tot.astype(BF16), preferred_element_type=F32)
        return incl, pre, tot

    eq_f = equal.astype(F32)
    incl_e, pre_e, _ = prefix(eq_f)
    chosen = above | (equal & (pre_e + incl_e - eq_f < need))
    incl, pre, tot = prefix(chosen.astype(F32))
    seg_lo = pre[:, 0:1]
    seg_hi = seg_lo + tot[:, 0:1]
    incl_t = jnp.transpose(incl).astype(BF16)
    aff_t = jnp.transpose(a)
    lane_id = lax.broadcasted_iota(I32, (LANES, SLOT_BLOCK), 0).astype(F32)

    for jb in range(cap // SLOT_BLOCK):
        slot = (jb * SLOT_BLOCK + lax.broadcasted_iota(I32, (1, SLOT_BLOCK), 1)).astype(F32)
        seg_of = jnp.sum((seg_hi <= slot).astype(F32), axis=0, keepdims=True)
        onehot = ((seg_lo <= slot) & (slot < seg_hi)).astype(F32)
        rank = slot - jnp.sum(onehot * seg_lo, axis=0, keepdims=True)
        counts = jnp.dot(incl_t, onehot.astype(BF16), preferred_element_type=F32)
        lane_of = jnp.sum((counts <= rank).astype(F32), axis=0, keepdims=True)
        token = (seg_of * LANES + lane_of).astype(I32) + tok0
        idx_ref[0, :, jb * SLOT_BLOCK:(jb + 1) * SLOT_BLOCK] = token * SLABS
        vals = jnp.dot(aff_t, onehot, precision=lax.Precision.HIGHEST, preferred_element_type=F32)
        gate_ref[0, :, jb * SLOT_BLOCK:(jb + 1) * SLOT_BLOCK] = jnp.sum(
            jnp.where(lane_id == lane_of, vals, 0.0), axis=0, keepdims=True)


def _select(aff_group, cap, tok0):
    n = aff_group.shape[1]
    aff3 = aff_group.reshape(N_EXPERTS, n // LANES, LANES)
    thr = pl.pallas_call(
        functools.partial(_threshold_body, cap=cap),
        out_shape=jax.ShapeDtypeStruct((N_EXPERTS, 1, LANES), I32),
        compiler_params=pltpu.CompilerParams(vmem_limit_bytes=VMEM_LIMIT),
        name="expert_threshold",
    )(aff3)
    spec = pl.BlockSpec((1, 1, cap), lambda e: (e, 0, 0))
    return pl.pallas_call(
        functools.partial(_select_body, cap=cap, tok0=tok0),
        grid=(N_EXPERTS,),
        in_specs=[pl.BlockSpec((1, n // LANES, LANES), lambda e: (e, 0, 0)),
                  pl.BlockSpec((1, 1, LANES), lambda e: (e, 0, 0))],
        out_specs=[spec, spec],
        out_shape=[jax.ShapeDtypeStruct((N_EXPERTS, 1, cap), I32), jax.ShapeDtypeStruct((N_EXPERTS, 1, cap), F32)],
        compiler_params=_params("parallel"),
        name="expert_select",
    )(aff3, thr)


def _ffn_body(idx_ref, h_hbm, gate_ref, wg_ref, wu_ref, wd_ref, x_in_hbm, x_hbm,
              hbuf_ref, xbuf_ref, xb_ref, acc_ref, sem_h, sem_x, sem_s, *, rows, n_tiles, steps):
    del x_in_hbm
    g = pl.program_id(0)
    f = pl.program_id(1)
    cur = g % 2
    oth = 1 - cur
    n_early = steps - 3
    per_early = rows // n_early
    per_late = rows // 2
    base = g * rows
    base_next = jnp.minimum(g + 1, n_tiles - 1) * rows
    base_prev = jnp.maximum(g - 1, 0) * rows

    def hbm_tok(hbm, tile_base, j):
        return hbm.at[pl.ds(pl.multiple_of(idx_ref[tile_base + j], SLABS), SLABS), :]

    def buf_tok(buf, slot, j):
        return buf.at[slot, pl.ds(pl.multiple_of(j * SLABS, SLABS), SLABS), :]

    def h_gather(tile_base, j, slot):
        return pltpu.make_async_copy(hbm_tok(h_hbm, tile_base, j), buf_tok(hbuf_ref, slot, j), sem_h.at[slot])

    def x_gather(tile_base, j, slot):
        return pltpu.make_async_copy(hbm_tok(x_hbm, tile_base, j), buf_tok(xbuf_ref, slot, j), sem_x.at[slot])

    def x_scatter(tile_base, j, slot):
        return pltpu.make_async_copy(buf_tok(xbuf_ref, slot, j), hbm_tok(x_hbm, tile_base, j), sem_s.at[slot])

    def wait_h(slot):
        pltpu.make_async_copy(h_hbm.at[pl.ds(0, SLABS * rows), :], hbuf_ref.at[slot], sem_h.at[slot]).wait()

    def wait_x(slot):
        pltpu.make_async_copy(x_hbm.at[pl.ds(0, SLABS * rows), :], xbuf_ref.at[slot], sem_x.at[slot]).wait()

    def wait_s(slot):
        pltpu.make_async_copy(xbuf_ref.at[slot], x_hbm.at[pl.ds(0, SLABS * rows), :], sem_s.at[slot]).wait()

    def stage_h(slot, row0, n):
        src = hbuf_ref.at[slot]
        for s in range(SLABS):
            slab = src[pl.ds(row0 * SLABS + s, n, stride=SLABS), :]
            xb_ref[slot, pl.ds(row0, n), s * LANES:(s + 1) * LANES] = slab.astype(BF16)

    def matmuls():
        xb = xb_ref[cur]
        hg = jnp.dot(xb, wg_ref[0, 0].astype(BF16), preferred_element_type=F32)
        hu = jnp.dot(xb, wu_ref[0, 0].astype(BF16), preferred_element_type=F32)
        hid = (hg * _sigmoid(hg) * hu).astype(BF16)
        acc_ref[...] += jnp.dot(hid, wd_ref[0, 0].astype(BF16), preferred_element_type=F32)

    @pl.when((g == 0) & (f == 0))
    def _():
        def issue(j, carry):
            h_gather(0, j, 0).start()
            x_gather(0, j, 1).start()
            return carry
        lax.fori_loop(0, rows, issue, 0)
        wait_h(0)
        wait_x(1)
        stage_h(0, 0, rows)

    @pl.when(f == 0)
    def _():
        acc_ref[...] = jnp.zeros_like(acc_ref)

    @pl.when(f < n_early)
    def _():
        j0 = f * per_early
        for i in range(per_early):
            h_gather(base_next, j0 + i, oth).start()
            x_scatter(base_prev, j0 + i, oth).start()
        matmuls()

    @pl.when(f == n_early)
    def _():
        wait_s(oth)
        wait_h(oth)

    @pl.when((f >= n_early) & (f < steps - 1))
    def _():
        j0 = pl.multiple_of((f - n_early) * per_late, per_late)
        for i in range(per_late):
            x_gather(base, j0 + i, cur).start()
        stage_h(oth, j0, per_late)
        matmuls()

    @pl.when(f == steps - 1)
    def _():
        matmuls()
        wait_x(cur)
        xrows = xbuf_ref.at[cur]
        gate = gate_ref[0]
        for s in range(SLABS):
            slab = pl.ds(s, rows, stride=SLABS)
            xrows[slab, :] = xrows[slab, :] + acc_ref[:, s * LANES:(s + 1) * LANES] * gate

    @pl.when((g == n_tiles - 1) & (f == steps - 1))
    def _():
        def issue(j, carry):
            x_scatter(base, j, cur).start()
            return carry
        lax.fori_loop(0, rows, issue, 0)
        wait_s(cur)


def _expert_ffn(idx_flat, h, gates, w_gate, w_up, w_down, layer, x):
    n_slots = gates.shape[1]
    d_exp = w_gate.shape[3]
    tiles = n_slots // FFN_ROWS
    steps = d_exp // FFN_COLS
    assert steps >= 4 and FFN_ROWS % (steps - 3) == 0 and FFN_ROWS % 2 == 0
    any_spec = pl.BlockSpec(memory_space=pl.ANY)
    grid_spec = pltpu.PrefetchScalarGridSpec(
        num_scalar_prefetch=1,
        grid=(N_EXPERTS * tiles, steps),
        in_specs=[
            any_spec,
            pl.BlockSpec((1, FFN_ROWS, 1), lambda g, f, idx: (g // tiles, g % tiles, 0)),
            pl.BlockSpec((1, 1, D_MODEL, FFN_COLS), lambda g, f, idx: (layer, g // tiles, 0, f)),
            pl.BlockSpec((1, 1, D_MODEL, FFN_COLS), lambda g, f, idx: (layer, g // tiles, 0, f)),
            pl.BlockSpec((1, 1, FFN_COLS, D_MODEL), lambda g, f, idx: (layer, g // tiles, f, 0)),
            any_spec,
        ],
        out_specs=any_spec,
        scratch_shapes=[pltpu.VMEM((2, SLABS * FFN_ROWS, LANES), F32), pltpu.VMEM((2, SLABS * FFN_ROWS, LANES), F32),
                        pltpu.VMEM((2, FFN_ROWS, D_MODEL), BF16), pltpu.VMEM((FFN_ROWS, D_MODEL), F32),
                        pltpu.SemaphoreType.DMA((2,)), pltpu.SemaphoreType.DMA((2,)),
                        pltpu.SemaphoreType.DMA((2,))],
    )
    return pl.pallas_call(
        functools.partial(_ffn_body, rows=FFN_ROWS, n_tiles=N_EXPERTS * tiles, steps=steps),
        grid_spec=grid_spec,
        out_shape=jax.ShapeDtypeStruct(x.shape, F32),
        input_output_aliases={6: 0},
        compiler_params=_params("arbitrary", "arbitrary"),
        name="expert_ffn",
    )(idx_flat, h, gates, w_gate, w_up, w_down, x)


def _final_norm_body(x_ref, g_ref, o_ref):
    x = _load_tokens(x_ref, ROW_TILE)
    o_ref[...] = x * lax.rsqrt(jnp.mean(x * x, axis=-1, keepdims=True) + EPS) * g_ref[...]


def _final_norm(xt, g, tok0, n):
    return pl.pallas_call(
        _final_norm_body,
        grid=(n // ROW_TILE,),
        in_specs=[_tiled_spec(ROW_TILE, tok0 // ROW_TILE), pl.BlockSpec(g.shape, lambda i: (0, 0))],
        out_specs=pl.BlockSpec((ROW_TILE, D_MODEL), lambda i: (i, 0)),
        out_shape=jax.ShapeDtypeStruct((n, D_MODEL), F32),
        compiler_params=_params("parallel"),
        name="final_norm",
    )(xt, g)


def _chunk_neighbours(seq_lens):
    has_prev, has_next = [], []
    for length in seq_lens:
        assert length % CHUNK == 0
        k = length // CHUNK
        has_prev += [int(i > 0) for i in range(k)]
        has_next += [int(i < k - 1) for i in range(k)]
    return jnp.asarray(has_prev, I32), jnp.asarray(has_next, I32)


def _trunk(x, groups, norm_mix_g, w_in, conv_w, conv_b, conv_ln_g, conv_ln_b, w_out, rel_bias,
           norm_ffn_g, w_router, w_gate, w_up, w_down, norm_final_g):
    seq_lens = [s for b, s in groups for _ in range(b)]
    has_prev, has_next = _chunk_neighbours(seq_lens)
    tables = _bias_tables(rel_bias)
    depth = w_in.shape[0]
    row2 = lambda a: a.reshape(1, -1)
    for l in range(depth):
        u, q, k, v = _mixer_in(x, row2(norm_mix_g[l]), w_in[l].astype(BF16))
        conv_out = _conformer_conv(u, has_prev, has_next, conv_w[l], row2(conv_b[l]),
                                   row2(conv_ln_g[l]), row2(conv_ln_b[l]))
        attn_out = _dilated_attention(q, k, v, has_prev, has_next, tables)
        wr = jnp.pad(w_router[l], ((0, 0), (0, LANES - N_EXPERTS)))
        wr_hi = wr.astype(BF16)
        wr_lo = (wr - wr_hi.astype(F32)).astype(BF16)
        x, h, aff = _mixer_out(conv_out, attn_out, x, w_out[l].astype(BF16), row2(norm_ffn_g[l]),
                               jnp.concatenate([wr_hi, wr_lo], axis=1))
        idx, gates, tok0 = [], [], 0
        for b, s in groups:
            n = b * s
            cap = CAPACITY_FACTOR * n // N_EXPERTS
            i, g = _select(aff[:, tok0:tok0 + n], cap, tok0)
            idx.append(i[:, 0, :])
            gates.append(g[:, 0, :])
            tok0 += n
        idx = jnp.concatenate(idx, axis=1)
        gates = jnp.concatenate(gates, axis=1)[:, :, None]
        x = _expert_ffn(idx.reshape(-1), h, gates, w_gate, w_up, w_down, l, x)
    outs, tok0 = [], 0
    for b, s in groups:
        outs.append(_final_norm(x, row2(norm_final_g), tok0, b * s).reshape(b, s, D_MODEL))
        tok0 += b * s
    return outs


def kernel(x_prompt, x_sample, norm_mix_g, w_in, conv_w, conv_b, conv_ln_g, conv_ln_b, w_out, rel_bias,
           norm_ffn_g, w_router, w_gate, w_up, w_down, norm_final_g):
    groups = [x_prompt.shape[:2], x_sample.shape[:2]]
    x = _row_tiled_concat(x_prompt.reshape(-1, D_MODEL), x_sample.reshape(-1, D_MODEL))
    y_prompt, y_sample = _trunk(x, groups, norm_mix_g, w_in, conv_w, conv_b, conv_ln_g, conv_ln_b, w_out,
                                rel_bias, norm_ffn_g, w_router, w_gate, w_up, w_down, norm_final_g)
    return y_prompt, y_sample
```

```python
import functools

import jax
import jax.numpy as jnp
import numpy as np
from jax import lax
from jax.experimental import pallas as pl
from jax.experimental.pallas import tpu as pltpu

F32 = jnp.float32
BF16 = jnp.bfloat16
I32 = jnp.int32

D_MODEL = 1024
D_CONV = 512
N_HEADS = 8
HEAD_DIM = 64
D_ATTN = N_HEADS * HEAD_DIM
CONV_WIDTH = 31
ATTN_DILATIONS = (1, 4, 16)
ATTN_HALF = 64
N_BUCKETS = 32
MAX_DISTANCE = 1024
N_EXPERTS = 16
CAPACITY_FACTOR = 2
EPS = 1e-6
NEG = -1e30
LOG2E = 1.4426950408889634

LANES = 128
QGROUP = 128
KGROUP = QGROUP + 2 * ATTN_HALF
CHUNK = QGROUP * max(ATTN_DILATIONS)
HALO = ATTN_HALF * max(ATTN_DILATIONS)
CONV_PAD = 16
ROW_TILE = 512
FFN_ROWS = 2048
FFN_COLS = 256
SLOT_BLOCK = 512
VMEM_LIMIT = 56 * 1024 * 1024


def _params(*sem):
    return pltpu.CompilerParams(dimension_semantics=sem, vmem_limit_bytes=VMEM_LIMIT)


def _sigmoid(x):
    return 1.0 / (1.0 + jnp.exp(-x))


SLABS = D_MODEL // LANES


def _load_tokens(ref, n):
    return jnp.concatenate([ref[pl.ds(s, n, stride=SLABS), :] for s in range(SLABS)], axis=1)


def _store_tokens(ref, val):
    n = val.shape[0]
    for s in range(SLABS):
        ref[pl.ds(s, n, stride=SLABS), :] = val[:, s * LANES:(s + 1) * LANES]


def _tiled_spec(rows, first=0):
    return pl.BlockSpec((SLABS * rows, LANES), lambda i, *_: (i + first, 0))


def _row_tiled_body(xa_ref, xb_ref, o_ref, *, tiles_a):
    i = pl.program_id(0)

    @pl.when(i < tiles_a)
    def _():
        _store_tokens(o_ref, xa_ref[...])

    @pl.when(i >= tiles_a)
    def _():
        _store_tokens(o_ref, xb_ref[...])


def _row_tiled_concat(xa, xb):
    tiles_a = xa.shape[0] // ROW_TILE
    tiles_b = xb.shape[0] // ROW_TILE
    n = xa.shape[0] + xb.shape[0]
    return pl.pallas_call(
        functools.partial(_row_tiled_body, tiles_a=tiles_a),
        grid=(tiles_a + tiles_b,),
        in_specs=[pl.BlockSpec((ROW_TILE, D_MODEL), lambda i: (jnp.minimum(i, tiles_a - 1), 0)),
                  pl.BlockSpec((ROW_TILE, D_MODEL), lambda i: (jnp.maximum(i - tiles_a, 0), 0))],
        out_specs=_tiled_spec(ROW_TILE),
        out_shape=jax.ShapeDtypeStruct((SLABS * n, LANES), F32),
        compiler_params=_params("parallel"),
        name="row_tiled_concat",
    )(xa, xb)


def _mixer_in_body(x_ref, g_ref, w_ref, u_ref, q_ref, k_ref, v_ref):
    x = _load_tokens(x_ref, ROW_TILE)
    xn = (x * lax.rsqrt(jnp.mean(x * x, axis=-1, keepdims=True) + EPS) * g_ref[...]).astype(BF16)

    def proj(j, width):
        return jnp.dot(xn, w_ref[:, j:j + width], preferred_element_type=F32)

    val = proj(0, D_CONV)
    gate = proj(D_CONV, D_CONV)
    u_ref[...] = val * _sigmoid(gate)
    q_ref[...] = proj(2 * D_CONV, D_ATTN) * (HEAD_DIM ** -0.5 * LOG2E)
    k_ref[...] = proj(2 * D_CONV + D_ATTN, D_ATTN)
    v_ref[...] = proj(2 * D_CONV + 2 * D_ATTN, D_ATTN)


def _mixer_in(xt, g, w_in_bf16):
    n = xt.shape[0] // SLABS
    row = lambda w: pl.BlockSpec((ROW_TILE, w), lambda i: (i, 0))
    full = lambda a: pl.BlockSpec(a.shape, lambda i: (0,) * a.ndim)
    out = lambda w: jax.ShapeDtypeStruct((n, w), F32)
    return pl.pallas_call(
        _mixer_in_body,
        grid=(n // ROW_TILE,),
        in_specs=[_tiled_spec(ROW_TILE), full(g), full(w_in_bf16)],
        out_specs=[row(D_CONV), row(D_ATTN), row(D_ATTN), row(D_ATTN)],
        out_shape=[out(D_CONV), out(D_ATTN), out(D_ATTN), out(D_ATTN)],
        compiler_params=_params("parallel"),
        name="mixer_in",
    )(xt, g, w_in_bf16)


CONV_ROWS = 64
CONV_LANES = 256


def _conv_body(hp_ref, hn_ref, up_ref, uc_ref, un_ref, w_ref, b_ref, g_ref, b2_ref, o_ref, win_ref):
    c = pl.program_id(0)
    keep_prev = (hp_ref[c] > 0).astype(F32)
    keep_next = (hn_ref[c] > 0).astype(F32)
    win_ref[0:CONV_PAD, :] = up_ref[...] * keep_prev
    win_ref[CONV_PAD:CONV_PAD + CHUNK, :] = uc_ref[...]
    win_ref[CONV_PAD + CHUNK:, :] = un_ref[...] * keep_next
    first = CONV_PAD - CONV_WIDTH // 2
    span = CONV_ROWS + 2 * CONV_PAD

    def tile(i, carry):
        r0 = pl.multiple_of(i * CONV_ROWS, CONV_ROWS)
        halves = []
        for c0 in range(0, D_CONV, CONV_LANES):
            win = win_ref[pl.ds(r0, span), c0:c0 + CONV_LANES]
            acc = jnp.zeros((CONV_ROWS, CONV_LANES), F32)
            for phase in range(8):
                taps = [k for k in range(CONV_WIDTH) if (first + k) % 8 == phase]
                if not taps:
                    continue
                shifted = win if phase == 0 else pltpu.roll(win, span - phase, axis=0)
                for k in taps:
                    off = first + k - phase
                    acc = acc + shifted[off:off + CONV_ROWS, :] * w_ref[k:k + 1, c0:c0 + CONV_LANES]
            halves.append(acc)
        y = jnp.concatenate(halves, axis=1) + b_ref[...]
        mu = jnp.mean(y, axis=-1, keepdims=True)
        yc = y - mu
        var = jnp.mean(yc * yc, axis=-1, keepdims=True)
        yn = yc * lax.rsqrt(var + EPS) * g_ref[...] + b2_ref[...]
        o_ref[pl.ds(r0, CONV_ROWS), :] = (yn * _sigmoid(yn)).astype(BF16)
        return carry

    lax.fori_loop(0, CHUNK // CONV_ROWS, tile, 0)


def _conformer_conv(u, has_prev, has_next, conv_w, conv_b, ln_g, ln_b):
    n = u.shape[0]
    n_chunks = n // CHUNK
    per = CHUNK // CONV_PAD
    last = n // CONV_PAD - 1
    full = lambda a: pl.BlockSpec(a.shape, lambda c, hp, hn: (0,) * a.ndim)
    grid_spec = pltpu.PrefetchScalarGridSpec(
        num_scalar_prefetch=2,
        grid=(n_chunks,),
        in_specs=[
            pl.BlockSpec((CONV_PAD, D_CONV), lambda c, hp, hn: (jnp.maximum(c * per - 1, 0), 0)),
            pl.BlockSpec((CHUNK, D_CONV), lambda c, hp, hn: (c, 0)),
            pl.BlockSpec((CONV_PAD, D_CONV), lambda c, hp, hn: (jnp.minimum((c + 1) * per, last), 0)),
            full(conv_w), full(conv_b), full(ln_g), full(ln_b),
        ],
        out_specs=pl.BlockSpec((CHUNK, D_CONV), lambda c, hp, hn: (c, 0)),
        scratch_shapes=[pltpu.VMEM((CHUNK + 2 * CONV_PAD, D_CONV), F32)],
    )
    return pl.pallas_call(
        _conv_body,
        grid_spec=grid_spec,
        out_shape=jax.ShapeDtypeStruct((n, D_CONV), BF16),
        compiler_params=_params("parallel"),
        name="conformer_conv",
    )(has_prev, has_next, u, u, u, conv_w, conv_b, ln_g, ln_b)


def _t5_bucket(rel):
    half = N_BUCKETS // 2
    max_exact = half // 2
    ret = np.where(rel > 0, half, 0)
    n = np.abs(rel)
    large = max_exact + (np.log(np.maximum(n, 1) / max_exact)
                         / np.log(MAX_DISTANCE / max_exact) * (half - max_exact)).astype(np.int32)
    large = np.minimum(large, half - 1)
    return (ret + np.where(n < max_exact, n, large)).astype(np.int32)


def _bias_tables(rel_bias):
    delta = np.arange(QGROUP + KGROUP - 1) - (QGROUP - 1) - ATTN_HALF
    band = np.abs(delta) <= ATTN_HALF
    tabs = []
    for d in ATTN_DILATIONS:
        onehot = np.eye(N_BUCKETS, dtype=np.float32)[_t5_bucket(d * delta)]
        base = jnp.dot(onehot, rel_bias.astype(F32), precision=lax.Precision.HIGHEST)
        base = jnp.where(band[:, None], base * LOG2E, NEG)
        rows = [base[QGROUP - 1 - i:QGROUP - 1 - i + KGROUP] for i in range(QGROUP)]
        b = jnp.transpose(jnp.stack(rows, axis=0), (2, 0, 1))
        tabs.append(b.reshape(N_HEADS // 2, 2 * QGROUP, KGROUP))
    return jnp.stack(tabs, axis=0)


MERGE_ROWS = 256


def _attn_body(hp_ref, hn_ref, q_ref, kp_ref, kc_ref, kn_ref, vp_ref, vc_ref, vn_ref, tab_ref, o_ref,
               kw_ref, vw_ref, *part_refs):
    nb = len(ATTN_DILATIONS)
    ob_refs, mb_refs, lb_refs = part_refs[:nb], part_refs[nb:2 * nb], part_refs[2 * nb:]
    c = pl.program_id(0)
    has_prev = hp_ref[c] > 0
    has_next = hn_ref[c] > 0
    kw_ref[0:HALO, :] = kp_ref[...]
    kw_ref[HALO:HALO + CHUNK, :] = kc_ref[...]
    kw_ref[HALO + CHUNK:, :] = kn_ref[...]
    vw_ref[0:HALO, :] = vp_ref[...]
    vw_ref[HALO:HALO + CHUNK, :] = vc_ref[...]
    vw_ref[HALO + CHUNK:, :] = vn_ref[...]

    first_head = lax.broadcasted_iota(I32, (QGROUP, LANES), 1) < HEAD_DIM
    key_pos = lax.broadcasted_iota(I32, (1, KGROUP), 1)

    for bi, d in enumerate(ATTN_DILATIONS):
        stride = None if d == 1 else d
        for g in range(CHUNK // QGROUP):
            res, blk = g % d, g // d
            q0 = blk * (QGROUP * d) + res
            k0 = HALO + (blk * QGROUP - ATTN_HALF) * d + res
            k_last = k0 + d * (KGROUP - 1)
            q = q_ref[pl.ds(q0, QGROUP, stride=stride), :]
            k = kw_ref[pl.ds(k0, KGROUP, stride=stride), :]
            v = vw_ref[pl.ds(k0, KGROUP, stride=stride), :]
            q2 = jnp.concatenate([jnp.where(first_head, q, 0.0), jnp.where(first_head, 0.0, q)], axis=0)
            s = lax.dot_general(q2.astype(BF16), k.astype(BF16), (((1,), (1,)), ((), ())),
                                preferred_element_type=F32)
            s = s + tab_ref[bi, 0]
            row = k0 + d * key_pos
            if k0 < HALO:
                s = jnp.where((row >= HALO) | has_prev, s, NEG)
            if k_last >= HALO + CHUNK:
                s = jnp.where((row < HALO + CHUNK) | has_next, s, NEG)
            m = jnp.max(s, axis=-1, keepdims=True)
            p = jnp.exp2(s - m)
            l = jnp.sum(p, axis=-1, keepdims=True)
            o = jnp.dot(p.astype(BF16), v.astype(BF16), preferred_element_type=F32)
            rows = pl.ds(q0, QGROUP, stride=stride)
            ob_refs[bi][rows, :] = jnp.where(first_head, o[:QGROUP], o[QGROUP:])
            mb_refs[bi][rows, :] = jnp.where(first_head, m[:QGROUP], m[QGROUP:])
            lb_refs[bi][rows, :] = jnp.where(first_head, l[:QGROUP], l[QGROUP:])

    def merge(t, carry):
        rows = pl.ds(pl.multiple_of(t * MERGE_ROWS, MERGE_ROWS), MERGE_ROWS)
        ms = [r[rows, :] for r in mb_refs]
        m_tot = functools.reduce(jnp.maximum, ms)
        ws = [jnp.exp2(m - m_tot) for m in ms]
        num = sum(w * r[rows, :] for w, r in zip(ws, ob_refs))
        den = sum(w * r[rows, :] for w, r in zip(ws, lb_refs))
        o_ref[rows, :] = (num / den).astype(BF16)
        return carry

    lax.fori_loop(0, CHUNK // MERGE_ROWS, merge, 0)


def _dilated_attention(q, k, v, has_prev, has_next, tables):
    n = q.shape[0]
    n_chunks = n // CHUNK
    per = CHUNK // HALO
    last = n // HALO - 1
    cur = pl.BlockSpec((CHUNK, LANES), lambda c, h, hp, hn: (c, h))
    prev = pl.BlockSpec((HALO, LANES), lambda c, h, hp, hn: (jnp.maximum(c * per - 1, 0), h))
    nxt = pl.BlockSpec((HALO, LANES), lambda c, h, hp, hn: (jnp.minimum((c + 1) * per, last), h))
    grid_spec = pltpu.PrefetchScalarGridSpec(
        num_scalar_prefetch=2,
        grid=(n_chunks, D_ATTN // LANES),
        in_specs=[cur, prev, cur, nxt, prev, cur, nxt,
                  pl.BlockSpec((len(ATTN_DILATIONS), 1, 2 * QGROUP, KGROUP), lambda c, h, hp, hn: (0, h, 0, 0))],
        out_specs=cur,
        scratch_shapes=([pltpu.VMEM((CHUNK + 2 * HALO, LANES), F32)] * 2
                        + [pltpu.VMEM((CHUNK, LANES), F32)] * (3 * len(ATTN_DILATIONS))),
    )
    return pl.pallas_call(
        _attn_body,
        grid_spec=grid_spec,
        out_shape=jax.ShapeDtypeStruct((n, D_ATTN), BF16),
        compiler_params=_params("parallel", "parallel"),
        name="dilated_attention",
    )(has_prev, has_next, q, k, k, k, v, v, v, tables)


def _mixer_out_body(cv_ref, at_ref, x_ref, w_ref, g_ref, wr_ref, xo_ref, h_ref, aff_ref):
    y = (jnp.dot(cv_ref[...], w_ref[0:D_CONV, :], preferred_element_type=F32)
         + jnp.dot(at_ref[...], w_ref[D_CONV:, :], preferred_element_type=F32))
    x = _load_tokens(x_ref, ROW_TILE) + y
    _store_tokens(xo_ref, x)
    h = x * lax.rsqrt(jnp.mean(x * x, axis=-1, keepdims=True) + EPS) * g_ref[...]
    _store_tokens(h_ref, h)
    h_hi = h.astype(BF16)
    h_lo = (h - h_hi.astype(F32)).astype(BF16)
    both = jnp.dot(h_hi, wr_ref[...], preferred_element_type=F32)
    low = jnp.dot(h_lo, wr_ref[:, :LANES], preferred_element_type=F32)
    logits = both[:, :LANES] + both[:, LANES:] + low
    logits = jnp.transpose(logits)[:N_EXPERTS, :]
    e = jnp.exp(logits - jnp.max(logits, axis=0, keepdims=True))
    aff_ref[...] = e / jnp.sum(e, axis=0, keepdims=True)


def _mixer_out(conv_out, attn_out, xt, w_out_bf16, g, w_router_pad):
    n = xt.shape[0] // SLABS
    row = lambda w: pl.BlockSpec((ROW_TILE, w), lambda i: (i, 0))
    full = lambda a: pl.BlockSpec(a.shape, lambda i: (0,) * a.ndim)
    tiled = jax.ShapeDtypeStruct(xt.shape, F32)
    return pl.pallas_call(
        _mixer_out_body,
        grid=(n // ROW_TILE,),
        in_specs=[row(D_CONV), row(D_ATTN), _tiled_spec(ROW_TILE), full(w_out_bf16), full(g), full(w_router_pad)],
        out_specs=[_tiled_spec(ROW_TILE), _tiled_spec(ROW_TILE),
                   pl.BlockSpec((N_EXPERTS, ROW_TILE), lambda i: (0, i))],
        out_shape=[tiled, tiled, jax.ShapeDtypeStruct((N_EXPERTS, n), F32)],
        input_output_aliases={2: 0},
        compiler_params=_params("parallel"),
        name="mixer_out_router",
    )(conv_out, attn_out, xt, w_out_bf16, g, w_router_pad)


def _count(mask):
    return jnp.sum(jnp.sum(mask.astype(F32), axis=0, keepdims=True), axis=1, keepdims=True)


def _threshold_body(aff_ref, thr_ref, *, cap):
    thr = [jnp.zeros((1, 1), I32) for _ in range(N_EXPERTS)]
    for bit in range(29, -1, -1):
        for e in range(N_EXPERTS):
            bits = pltpu.bitcast(aff_ref[e], I32)
            cand = thr[e] | (1 << bit)
            thr[e] = jnp.where(_count(bits >= cand) >= cap, cand, thr[e])
    for e in range(N_EXPERTS):
        thr_ref[e] = jnp.broadcast_to(thr[e], (1, LANES))


def _select_body(aff_ref, thr_ref, idx_ref, gate_ref, *, cap, tok0):
    a = aff_ref[0]
    segs = a.shape[0]
    bits = pltpu.bitcast(a, I32)
    count = _count
    thr = thr_ref[0][:, 0:1]
    above = bits > thr
    equal = bits == thr
    need = cap - count(above)

    lane_r = lax.broadcasted_iota(I32, (LANES, LANES), 0)
    lane_c = lax.broadcasted_iota(I32, (LANES, LANES), 1)
    upper = (lane_r <= lane_c).astype(BF16)
    seg_r = lax.broadcasted_iota(I32, (segs, segs), 0)
    seg_c = lax.broadcasted_iota(I32, (segs, segs), 1)
    before = (seg_c < seg_r).astype(BF16)

    def prefix(mask):
        incl = jnp.dot(mask.astype(BF16), upper, preferred_element_type=F32)
        tot = jnp.broadcast_to(incl[:, LANES - 1:LANES], (segs, LANES))
        pre = jnp.dot(before, tot.astype(BF16), preferred_element_type=F32)
        return incl, pre, tot

    eq_f = equal.astype(F32)
    incl_e, pre_e, _ = prefix(eq_f)
    chosen = above | (equal & (pre_e + incl_e - eq_f < need))
    incl, pre, tot = prefix(chosen.astype(F32))
    seg_lo = pre[:, 0:1]
    seg_hi = seg_lo + tot[:, 0:1]
    incl_t = jnp.transpose(incl).astype(BF16)
    aff_t = jnp.transpose(a)
    lane_id = lax.broadcasted_iota(I32, (LANES, SLOT_BLOCK), 0).astype(F32)

    for jb in range(cap // SLOT_BLOCK):
        slot = (jb * SLOT_BLOCK + lax.broadcasted_iota(I32, (1, SLOT_BLOCK), 1)).astype(F32)
        seg_of = jnp.sum((seg_hi <= slot).astype(F32), axis=0, keepdims=True)
        onehot = ((seg_lo <= slot) & (slot < seg_hi)).astype(F32)
        rank = slot - jnp.sum(onehot * seg_lo, axis=0, keepdims=True)
        counts = jnp.dot(incl_t, onehot.astype(BF16), preferred_element_type=F32)
        lane_of = jnp.sum((counts <= rank).astype(F32), axis=0, keepdims=True)
        token = (seg_of * LANES + lane_of).astype(I32) + tok0
        idx_ref[0, :, jb * SLOT_BLOCK:(jb + 1) * SLOT_BLOCK] = token * SLABS
        vals = jnp.dot(aff_t, onehot, precision=lax.Precision.HIGHEST, preferred_element_type=F32)
        gate_ref[0, :, jb * SLOT_BLOCK:(jb + 1) * SLOT_BLOCK] = jnp.sum(
            jnp.where(lane_id == lane_of, vals, 0.0), axis=0, keepdims=True)


def _select(aff_group, cap, tok0):
    n = aff_group.shape[1]
    aff3 = aff_group.reshape(N_EXPERTS, n // LANES, LANES)
    thr = pl.pallas_call(
        functools.partial(_threshold_body, cap=cap),
        out_shape=jax.ShapeDtypeStruct((N_EXPERTS, 1, LANES), I32),
        compiler_params=pltpu.CompilerParams(vmem_limit_bytes=VMEM_LIMIT),
        name="expert_threshold",
    )(aff3)
    spec = pl.BlockSpec((1, 1, cap), lambda e: (e, 0, 0))
    return pl.pallas_call(
        functools.partial(_select_body, cap=cap, tok0=tok0),
        grid=(N_EXPERTS,),
        in_specs=[pl.BlockSpec((1, n // LANES, LANES), lambda e: (e, 0, 0)),
                  pl.BlockSpec((1, 1, LANES), lambda e: (e, 0, 0))],
        out_specs=[spec, spec],
        out_shape=[jax.ShapeDtypeStruct((N_EXPERTS, 1, cap), I32), jax.ShapeDtypeStruct((N_EXPERTS, 1, cap), F32)],
        compiler_params=_params("parallel"),
        name="expert_select",
    )(aff3, thr)


def _ffn_body(idx_ref, h_hbm, gate_ref, wg_ref, wu_ref, wd_ref, x_in_hbm, x_hbm,
              hbuf_ref, xbuf_ref, xb_ref, acc_ref, sem_h, sem_x, sem_s, *, rows, n_tiles, steps):
    del x_in_hbm
    g = pl.program_id(0)
    f = pl.program_id(1)
    cur = g % 2
    oth = 1 - cur
    n_early = steps - 3
    per_early = rows // n_early
    per_late = rows // 2
    base = g * rows
    base_next = jnp.minimum(g + 1, n_tiles - 1) * rows
    base_prev = jnp.maximum(g - 1, 0) * rows

    def hbm_tok(hbm, tile_base, j):
        return hbm.at[pl.ds(pl.multiple_of(idx_ref[tile_base + j], SLABS), SLABS), :]

    def buf_tok(buf, j):
        return buf.at[pl.ds(pl.multiple_of(j * SLABS, SLABS), SLABS), :]

    def h_gather(tile_base, j):
        return pltpu.make_async_copy(hbm_tok(h_hbm, tile_base, j), buf_tok(hbuf_ref, j), sem_h)

    def x_gather(tile_base, j):
        return pltpu.make_async_copy(hbm_tok(x_hbm, tile_base, j), buf_tok(xbuf_ref, j), sem_x)

    def x_scatter(tile_base, j):
        return pltpu.make_async_copy(buf_tok(xbuf_ref, j), hbm_tok(x_hbm, tile_base, j), sem_s)

    def wait_h():
        pltpu.make_async_copy(h_hbm.at[pl.ds(0, SLABS * rows), :], hbuf_ref, sem_h).wait()

    def wait_x():
        pltpu.make_async_copy(x_hbm.at[pl.ds(0, SLABS * rows), :], xbuf_ref, sem_x).wait()

    def wait_s():
        pltpu.make_async_copy(xbuf_ref, x_hbm.at[pl.ds(0, SLABS * rows), :], sem_s).wait()

    def stage_h(slot, row0, n):
        for s in range(SLABS):
            slab = hbuf_ref[pl.ds(row0 * SLABS + s, n, stride=SLABS), :]
            xb_ref[slot, pl.ds(row0, n), s * LANES:(s + 1) * LANES] = slab.astype(BF16)

    def matmuls(first=False):
        xb = xb_ref[cur]
        hg = jnp.dot(xb, wg_ref[0, 0].astype(BF16), preferred_element_type=F32)
        hu = jnp.dot(xb, wu_ref[0, 0].astype(BF16), preferred_element_type=F32)
        hid = (hg * _sigmoid(hg) * hu).astype(BF16)
        y = jnp.dot(hid, wd_ref[0, 0].astype(BF16), preferred_element_type=F32)
        if first:
            acc_ref[...] = y
        else:
            acc_ref[...] += y

    @pl.when((g == 0) & (f == 0))
    def _():
        def issue(j, carry):
            h_gather(0, j).start()
            x_gather(0, j).start()
            return carry
        lax.fori_loop(0, rows, issue, 0)
        wait_h()
        wait_x()
        stage_h(0, 0, rows)

    def early(first):
        j0 = f * per_early
        for i in range(per_early):
            h_gather(base_next, j0 + i).start()
            x_scatter(base_prev, j0 + i).start()
        matmuls(first)

    @pl.when(f == 0)
    def _():
        early(True)

    @pl.when((f > 0) & (f < n_early))
    def _():
        early(False)

    @pl.when(f == n_early)
    def _():
        wait_s()
        wait_h()

    @pl.when((f >= n_early) & (f < steps - 1))
    def _():
        j0 = pl.multiple_of((f - n_early) * per_late, per_late)
        for i in range(per_late):
            x_gather(base, j0 + i).start()
        stage_h(oth, j0, per_late)
        matmuls()

    @pl.when(f == steps - 1)
    def _():
        matmuls()
        wait_x()
        gate = gate_ref[0]
        for s in range(SLABS):
            slab = pl.ds(s, rows, stride=SLABS)
            xbuf_ref[slab, :] = xbuf_ref[slab, :] + acc_ref[:, s * LANES:(s + 1) * LANES] * gate

    @pl.when((g == n_tiles - 1) & (f == steps - 1))
    def _():
        def issue(j, carry):
            x_scatter(base, j).start()
            return carry
        lax.fori_loop(0, rows, issue, 0)
        wait_s()


def _expert_ffn(idx_flat, h, gates, w_gate, w_up, w_down, layer, x):
    n_slots = gates.shape[1]
    d_exp = w_gate.shape[3]
    tiles = n_slots // FFN_ROWS
    steps = d_exp // FFN_COLS
    assert steps >= 4 and FFN_ROWS % (steps - 3) == 0 and FFN_ROWS % 2 == 0
    any_spec = pl.BlockSpec(memory_space=pl.ANY)
    grid_spec = pltpu.PrefetchScalarGridSpec(
        num_scalar_prefetch=1,
        grid=(N_EXPERTS * tiles, steps),
        in_specs=[
            any_spec,
            pl.BlockSpec((1, FFN_ROWS, 1), lambda g, f, idx: (g // tiles, g % tiles, 0)),
            pl.BlockSpec((1, 1, D_MODEL, FFN_COLS), lambda g, f, idx: (layer, g // tiles, 0, f)),
            pl.BlockSpec((1, 1, D_MODEL, FFN_COLS), lambda g, f, idx: (layer, g // tiles, 0, f)),
            pl.BlockSpec((1, 1, FFN_COLS, D_MODEL), lambda g, f, idx: (layer, g // tiles, f, 0)),
            any_spec,
        ],
        out_specs=any_spec,
        scratch_shapes=[pltpu.VMEM((SLABS * FFN_ROWS, LANES), F32), pltpu.VMEM((SLABS * FFN_ROWS, LANES), F32),
                        pltpu.VMEM((2, FFN_ROWS, D_MODEL), BF16), pltpu.VMEM((FFN_ROWS, D_MODEL), F32),
                        pltpu.SemaphoreType.DMA, pltpu.SemaphoreType.DMA, pltpu.SemaphoreType.DMA],
    )
    return pl.pallas_call(
        functools.partial(_ffn_body, rows=FFN_ROWS, n_tiles=N_EXPERTS * tiles, steps=steps),
        grid_spec=grid_spec,
        out_shape=jax.ShapeDtypeStruct(x.shape, F32),
        input_output_aliases={6: 0},
        compiler_params=_params("arbitrary", "arbitrary"),
        name="expert_ffn",
    )(idx_flat, h, gates, w_gate, w_up, w_down, x)


def _final_norm_body(x_ref, g_ref, o_ref):
    x = _load_tokens(x_ref, ROW_TILE)
    o_ref[...] = x * lax.rsqrt(jnp.mean(x * x, axis=-1, keepdims=True) + EPS) * g_ref[...]


def _final_norm(xt, g, tok0, n):
    return pl.pallas_call(
        _final_norm_body,
        grid=(n // ROW_TILE,),
        in_specs=[_tiled_spec(ROW_TILE, tok0 // ROW_TILE), pl.BlockSpec(g.shape, lambda i: (0, 0))],
        out_specs=pl.BlockSpec((ROW_TILE, D_MODEL), lambda i: (i, 0)),
        out_shape=jax.ShapeDtypeStruct((n, D_MODEL), F32),
        compiler_params=_params("parallel"),
        name="final_norm",
    )(xt, g)


def _chunk_neighbours(seq_lens):
    has_prev, has_next = [], []
    for length in seq_lens:
        assert length % CHUNK == 0
        k = length // CHUNK
        has_prev += [int(i > 0) for i in range(k)]
        has_next += [int(i < k - 1) for i in range(k)]
    return jnp.asarray(has_prev, I32), jnp.asarray(has_next, I32)


def _trunk(x, groups, norm_mix_g, w_in, conv_w, conv_b, conv_ln_g, conv_ln_b, w_out, rel_bias,
           norm_ffn_g, w_router, w_gate, w_up, w_down, norm_final_g):
    seq_lens = [s for b, s in groups for _ in range(b)]
    has_prev, has_next = _chunk_neighbours(seq_lens)
    tables = _bias_tables(rel_bias)
    depth = w_in.shape[0]
    row2 = lambda a: a.reshape(1, -1)
    for l in range(depth):
        u, q, k, v = _mixer_in(x, row2(norm_mix_g[l]), w_in[l].astype(BF16))
        conv_out = _conformer_conv(u, has_prev, has_next, conv_w[l], row2(conv_b[l]),
                                   row2(conv_ln_g[l]), row2(conv_ln_b[l]))
        attn_out = _dilated_attention(q, k, v, has_prev, has_next, tables)
        wr = jnp.pad(w_router[l], ((0, 0), (0, LANES - N_EXPERTS)))
        wr_hi = wr.astype(BF16)
        wr_lo = (wr - wr_hi.astype(F32)).astype(BF16)
        x, h, aff = _mixer_out(conv_out, attn_out, x, w_out[l].astype(BF16), row2(norm_ffn_g[l]),
                               jnp.concatenate([wr_hi, wr_lo], axis=1))
        idx, gates, tok0 = [], [], 0
        for b, s in groups:
            n = b * s
            cap = CAPACITY_FACTOR * n // N_EXPERTS
            i, g = _select(aff[:, tok0:tok0 + n], cap, tok0)
            idx.append(i[:, 0, :])
            gates.append(g[:, 0, :])
            tok0 += n
        idx = jnp.concatenate(idx, axis=1)
        gates = jnp.concatenate(gates, axis=1)[:, :, None]
        x = _expert_ffn(idx.reshape(-1), h, gates, w_gate, w_up, w_down, l, x)
    outs, tok0 = [], 0
    for b, s in groups:
        outs.append(_final_norm(x, row2(norm_final_g), tok0, b * s).reshape(b, s, D_MODEL))
        tok0 += b * s
    return outs


def kernel(x_prompt, x_sample, norm_mix_g, w_in, conv_w, conv_b, conv_ln_g, conv_ln_b, w_out, rel_bias,
           norm_ffn_g, w_router, w_gate, w_up, w_down, norm_final_g):
    groups = [x_prompt.shape[:2], x_sample.shape[:2]]
    x = _row_tiled_concat(x_prompt.reshape(-1, D_MODEL), x_sample.reshape(-1, D_MODEL))
    y_prompt, y_sample = _trunk(x, groups, norm_mix_g, w_in, conv_w, conv_b, conv_ln_g, conv_ln_b, w_out,
                                rel_bias, norm_ffn_g, w_router, w_gate, w_up, w_down, norm_final_g)
    return y_prompt, y_sample
```

```python
import functools

import jax
import jax.numpy as jnp
import numpy as np
from jax import lax
from jax.experimental import pallas as pl
from jax.experimental.pallas import tpu as pltpu

F32 = jnp.float32
BF16 = jnp.bfloat16
I32 = jnp.int32

D_MODEL = 1024
D_CONV = 512
N_HEADS = 8
HEAD_DIM = 64
D_ATTN = N_HEADS * HEAD_DIM
CONV_WIDTH = 31
ATTN_DILATIONS = (1, 4, 16)
ATTN_HALF = 64
N_BUCKETS = 32
MAX_DISTANCE = 1024
N_EXPERTS = 16
CAPACITY_FACTOR = 2
EPS = 1e-6
NEG = -1e30
LOG2E = 1.4426950408889634

LANES = 128
QGROUP = 128
KGROUP = QGROUP + 2 * ATTN_HALF
CHUNK = QGROUP * max(ATTN_DILATIONS)
HALO = ATTN_HALF * max(ATTN_DILATIONS)
CONV_PAD = 16
ROW_TILE = 512
FFN_ROWS = 1024
FFN_COLS = 256
SLOT_BLOCK = 512
VMEM_LIMIT = 56 * 1024 * 1024


def _params(*sem):
    return pltpu.CompilerParams(dimension_semantics=sem, vmem_limit_bytes=VMEM_LIMIT)


def _sigmoid(x):
    return 1.0 / (1.0 + jnp.exp(-x))


SLABS = D_MODEL // LANES


def _load_tokens(ref, n):
    return jnp.concatenate([ref[pl.ds(s, n, stride=SLABS), :] for s in range(SLABS)], axis=1)


def _store_tokens(ref, val):
    n = val.shape[0]
    for s in range(SLABS):
        ref[pl.ds(s, n, stride=SLABS), :] = val[:, s * LANES:(s + 1) * LANES]


def _tiled_spec(rows, first=0):
    return pl.BlockSpec((SLABS * rows, LANES), lambda i, *_: (i + first, 0))


def _row_tiled_body(xa_ref, xb_ref, o_ref, *, tiles_a):
    i = pl.program_id(0)

    @pl.when(i < tiles_a)
    def _():
        _store_tokens(o_ref, xa_ref[...])

    @pl.when(i >= tiles_a)
    def _():
        _store_tokens(o_ref, xb_ref[...])


def _row_tiled_concat(xa, xb):
    tiles_a = xa.shape[0] // ROW_TILE
    tiles_b = xb.shape[0] // ROW_TILE
    n = xa.shape[0] + xb.shape[0]
    return pl.pallas_call(
        functools.partial(_row_tiled_body, tiles_a=tiles_a),
        grid=(tiles_a + tiles_b,),
        in_specs=[pl.BlockSpec((ROW_TILE, D_MODEL), lambda i: (jnp.minimum(i, tiles_a - 1), 0)),
                  pl.BlockSpec((ROW_TILE, D_MODEL), lambda i: (jnp.maximum(i - tiles_a, 0), 0))],
        out_specs=_tiled_spec(ROW_TILE),
        out_shape=jax.ShapeDtypeStruct((SLABS * n, LANES), F32),
        compiler_params=_params("parallel"),
        name="row_tiled_concat",
    )(xa, xb)


CONV_ROWS = 64
CONV_LANES = 256


def _mixer_in_body(hp_ref, hn_ref, x_ref, g_ref, w_ref, cw_ref, cb_ref, lg_ref, lb_ref,
                   q_ref, k_ref, v_ref, c_ref, win_ref, u_ref, *, n_tiles):
    i = pl.program_id(0)
    prev_tile = jnp.maximum(i - 1, 0)
    this_tile = jnp.minimum(i, n_tiles - 1)

    @pl.when(i == 0)
    def _():
        win_ref[...] = jnp.zeros_like(win_ref)

    x = _load_tokens(x_ref, ROW_TILE)
    xn = (x * lax.rsqrt(jnp.mean(x * x, axis=-1, keepdims=True) + EPS) * g_ref[...]).astype(BF16)

    def proj(j, width):
        return jnp.dot(xn, w_ref[:, j:j + width], preferred_element_type=F32)

    val = proj(0, D_CONV)
    gate = proj(D_CONV, D_CONV)
    u_ref[...] = val * _sigmoid(gate)
    q_ref[...] = proj(2 * D_CONV, D_ATTN) * (HEAD_DIM ** -0.5 * LOG2E)
    k_ref[...] = proj(2 * D_CONV + D_ATTN, D_ATTN)
    v_ref[...] = proj(2 * D_CONV + 2 * D_ATTN, D_ATTN)

    body = CONV_PAD + ROW_TILE
    win_ref[body:, :] = u_ref[0:CONV_PAD, :] * (hn_ref[prev_tile] > 0).astype(F32)
    first = CONV_PAD - CONV_WIDTH // 2
    span = CONV_ROWS + 2 * CONV_PAD
    for r0 in range(0, ROW_TILE, CONV_ROWS):
        halves = []
        for c0 in range(0, D_CONV, CONV_LANES):
            win = win_ref[r0:r0 + span, c0:c0 + CONV_LANES]
            acc = jnp.zeros((CONV_ROWS, CONV_LANES), F32)
            for phase in range(8):
                taps = [k for k in range(CONV_WIDTH) if (first + k) % 8 == phase]
                if not taps:
                    continue
                shifted = win if phase == 0 else pltpu.roll(win, span - phase, axis=0)
                for k in taps:
                    off = first + k - phase
                    acc = acc + shifted[off:off + CONV_ROWS, :] * cw_ref[k:k + 1, c0:c0 + CONV_LANES]
            halves.append(acc)
        y = jnp.concatenate(halves, axis=1) + cb_ref[...]
        mu = jnp.mean(y, axis=-1, keepdims=True)
        yc = y - mu
        var = jnp.mean(yc * yc, axis=-1, keepdims=True)
        yn = yc * lax.rsqrt(var + EPS) * lg_ref[...] + lb_ref[...]
        c_ref[r0:r0 + CONV_ROWS, :] = (yn * _sigmoid(yn)).astype(BF16)

    win_ref[0:CONV_PAD, :] = win_ref[ROW_TILE:body, :] * (hp_ref[this_tile] > 0).astype(F32)
    win_ref[CONV_PAD:body, :] = u_ref[...]


def _mixer_in(xt, tile_prev, tile_next, g, w_in_bf16, conv_w, conv_b, ln_g, ln_b):
    n = xt.shape[0] // SLABS
    n_tiles = n // ROW_TILE
    cur = lambda i, hp, hn: (jnp.minimum(i, n_tiles - 1), 0)
    full = lambda a: pl.BlockSpec(a.shape, lambda i, hp, hn: (0,) * a.ndim)
    row = pl.BlockSpec((ROW_TILE, D_ATTN), cur)
    grid_spec = pltpu.PrefetchScalarGridSpec(
        num_scalar_prefetch=2,
        grid=(n_tiles + 1,),
        in_specs=[pl.BlockSpec((SLABS * ROW_TILE, LANES), cur), full(g), full(w_in_bf16),
                  full(conv_w), full(conv_b), full(ln_g), full(ln_b)],
        out_specs=[row, row, row,
                   pl.BlockSpec((ROW_TILE, D_CONV), lambda i, hp, hn: (jnp.maximum(i - 1, 0), 0))],
        scratch_shapes=[pltpu.VMEM((ROW_TILE + 2 * CONV_PAD, D_CONV), F32), pltpu.VMEM((ROW_TILE, D_CONV), F32)],
    )
    return pl.pallas_call(
        functools.partial(_mixer_in_body, n_tiles=n_tiles),
        grid_spec=grid_spec,
        out_shape=[jax.ShapeDtypeStruct((n, D_ATTN), F32)] * 3 + [jax.ShapeDtypeStruct((n, D_CONV), BF16)],
        compiler_params=_params("arbitrary"),
        name="mixer_in_conv",
    )(tile_prev, tile_next, xt, g, w_in_bf16, conv_w, conv_b, ln_g, ln_b)


def _t5_bucket(rel):
    half = N_BUCKETS // 2
    max_exact = half // 2
    ret = np.where(rel > 0, half, 0)
    n = np.abs(rel)
    large = max_exact + (np.log(np.maximum(n, 1) / max_exact)
                         / np.log(MAX_DISTANCE / max_exact) * (half - max_exact)).astype(np.int32)
    large = np.minimum(large, half - 1)
    return (ret + np.where(n < max_exact, n, large)).astype(np.int32)


def _bias_tables(rel_bias):
    delta = np.arange(QGROUP + KGROUP - 1) - (QGROUP - 1) - ATTN_HALF
    band = np.abs(delta) <= ATTN_HALF
    tabs = []
    for d in ATTN_DILATIONS:
        onehot = np.eye(N_BUCKETS, dtype=np.float32)[_t5_bucket(d * delta)]
        base = jnp.dot(onehot, rel_bias.astype(F32), precision=lax.Precision.HIGHEST)
        base = jnp.where(band[:, None], base * LOG2E, NEG)
        rows = [base[QGROUP - 1 - i:QGROUP - 1 - i + KGROUP] for i in range(QGROUP)]
        b = jnp.transpose(jnp.stack(rows, axis=0), (2, 0, 1))
        tabs.append(b.reshape(N_HEADS // 2, 2 * QGROUP, KGROUP))
    return jnp.stack(tabs, axis=0)


MERGE_ROWS = 256


def _attn_body(hp_ref, hn_ref, q_ref, kp_ref, kc_ref, kn_ref, vp_ref, vc_ref, vn_ref, tab_ref, o_ref,
               kw_ref, vw_ref, *part_refs):
    nb = len(ATTN_DILATIONS)
    ob_refs, mb_refs, lb_refs = part_refs[:nb], part_refs[nb:2 * nb], part_refs[2 * nb:]
    c = pl.program_id(0)
    has_prev = hp_ref[c] > 0
    has_next = hn_ref[c] > 0
    kw_ref[0:HALO, :] = kp_ref[...]
    kw_ref[HALO:HALO + CHUNK, :] = kc_ref[...]
    kw_ref[HALO + CHUNK:, :] = kn_ref[...]
    vw_ref[0:HALO, :] = vp_ref[...]
    vw_ref[HALO:HALO + CHUNK, :] = vc_ref[...]
    vw_ref[HALO + CHUNK:, :] = vn_ref[...]

    first_head = lax.broadcasted_iota(I32, (QGROUP, LANES), 1) < HEAD_DIM
    key_pos = lax.broadcasted_iota(I32, (1, KGROUP), 1)

    for bi, d in enumerate(ATTN_DILATIONS):
        stride = None if d == 1 else d
        for g in range(CHUNK // QGROUP):
            res, blk = g % d, g // d
            q0 = blk * (QGROUP * d) + res
            k0 = HALO + (blk * QGROUP - ATTN_HALF) * d + res
            k_last = k0 + d * (KGROUP - 1)
            q = q_ref[pl.ds(q0, QGROUP, stride=stride), :]
            k = kw_ref[pl.ds(k0, KGROUP, stride=stride), :]
            v = vw_ref[pl.ds(k0, KGROUP, stride=stride), :]
            q2 = jnp.concatenate([jnp.where(first_head, q, 0.0), jnp.where(first_head, 0.0, q)], axis=0)
            s = lax.dot_general(q2.astype(BF16), k.astype(BF16), (((1,), (1,)), ((), ())),
                                preferred_element_type=F32)
            s = s + tab_ref[bi, 0]
            row = k0 + d * key_pos
            if k0 < HALO:
                s = jnp.where((row >= HALO) | has_prev, s, NEG)
            if k_last >= HALO + CHUNK:
                s = jnp.where((row < HALO + CHUNK) | has_next, s, NEG)
            m = jnp.max(s, axis=-1, keepdims=True)
            p = jnp.exp2(s - m)
            l = jnp.sum(p, axis=-1, keepdims=True)
            o = jnp.dot(p.astype(BF16), v.astype(BF16), preferred_element_type=F32)
            rows = pl.ds(q0, QGROUP, stride=stride)
            ob_refs[bi][rows, :] = jnp.where(first_head, o[:QGROUP], o[QGROUP:])
            mb_refs[bi][rows, :] = jnp.where(first_head, m[:QGROUP], m[QGROUP:])
            lb_refs[bi][rows, :] = jnp.where(first_head, l[:QGROUP], l[QGROUP:])

    def merge(t, carry):
        rows = pl.ds(pl.multiple_of(t * MERGE_ROWS, MERGE_ROWS), MERGE_ROWS)
        ms = [r[rows, :] for r in mb_refs]
        m_tot = functools.reduce(jnp.maximum, ms)
        ws = [jnp.exp2(m - m_tot) for m in ms]
        num = sum(w * r[rows, :] for w, r in zip(ws, ob_refs))
        den = sum(w * r[rows, :] for w, r in zip(ws, lb_refs))
        o_ref[rows, :] = (num / den).astype(BF16)
        return carry

    lax.fori_loop(0, CHUNK // MERGE_ROWS, merge, 0)


def _dilated_attention(q, k, v, has_prev, has_next, tables):
    n = q.shape[0]
    n_chunks = n // CHUNK
    per = CHUNK // HALO
    last = n // HALO - 1
    cur = pl.BlockSpec((CHUNK, LANES), lambda c, h, hp, hn: (c, h))
    prev = pl.BlockSpec((HALO, LANES), lambda c, h, hp, hn: (jnp.maximum(c * per - 1, 0), h))
    nxt = pl.BlockSpec((HALO, LANES), lambda c, h, hp, hn: (jnp.minimum((c + 1) * per, last), h))
    grid_spec = pltpu.PrefetchScalarGridSpec(
        num_scalar_prefetch=2,
        grid=(n_chunks, D_ATTN // LANES),
        in_specs=[cur, prev, cur, nxt, prev, cur, nxt,
                  pl.BlockSpec((len(ATTN_DILATIONS), 1, 2 * QGROUP, KGROUP), lambda c, h, hp, hn: (0, h, 0, 0))],
        out_specs=cur,
        scratch_shapes=([pltpu.VMEM((CHUNK + 2 * HALO, LANES), F32)] * 2
                        + [pltpu.VMEM((CHUNK, LANES), F32)] * (3 * len(ATTN_DILATIONS))),
    )
    return pl.pallas_call(
        _attn_body,
        grid_spec=grid_spec,
        out_shape=jax.ShapeDtypeStruct((n, D_ATTN), BF16),
        compiler_params=_params("parallel", "parallel"),
        name="dilated_attention",
    )(has_prev, has_next, q, k, k, k, v, v, v, tables)


def _mixer_out_body(cv_ref, at_ref, x_ref, w_ref, g_ref, wr_ref, xo_ref, h_ref, aff_ref):
    y = (jnp.dot(cv_ref[...], w_ref[0:D_CONV, :], preferred_element_type=F32)
         + jnp.dot(at_ref[...], w_ref[D_CONV:, :], preferred_element_type=F32))
    x = _load_tokens(x_ref, ROW_TILE) + y
    _store_tokens(xo_ref, x)
    h = x * lax.rsqrt(jnp.mean(x * x, axis=-1, keepdims=True) + EPS) * g_ref[...]
    _store_tokens(h_ref, h)
    h_hi = h.astype(BF16)
    h_lo = (h - h_hi.astype(F32)).astype(BF16)
    both = jnp.dot(h_hi, wr_ref[...], preferred_element_type=F32)
    low = jnp.dot(h_lo, wr_ref[:, :LANES], preferred_element_type=F32)
    logits = both[:, :LANES] + both[:, LANES:] + low
    logits = jnp.transpose(logits)[:N_EXPERTS, :]
    e = jnp.exp(logits - jnp.max(logits, axis=0, keepdims=True))
    aff_ref[...] = e / jnp.sum(e, axis=0, keepdims=True)


def _mixer_out(conv_out, attn_out, xt, w_out_bf16, g, w_router_pad):
    n = xt.shape[0] // SLABS
    row = lambda w: pl.BlockSpec((ROW_TILE, w), lambda i: (i, 0))
    full = lambda a: pl.BlockSpec(a.shape, lambda i: (0,) * a.ndim)
    tiled = jax.ShapeDtypeStruct(xt.shape, F32)
    return pl.pallas_call(
        _mixer_out_body,
        grid=(n // ROW_TILE,),
        in_specs=[row(D_CONV), row(D_ATTN), _tiled_spec(ROW_TILE), full(w_out_bf16), full(g), full(w_router_pad)],
        out_specs=[_tiled_spec(ROW_TILE), _tiled_spec(ROW_TILE),
                   pl.BlockSpec((N_EXPERTS, ROW_TILE), lambda i: (0, i))],
        out_shape=[tiled, tiled, jax.ShapeDtypeStruct((N_EXPERTS, n), F32)],
        input_output_aliases={2: 0},
        compiler_params=_params("parallel"),
        name="mixer_out_router",
    )(conv_out, attn_out, xt, w_out_bf16, g, w_router_pad)


def _count(mask):
    return jnp.sum(jnp.sum(mask.astype(F32), axis=0, keepdims=True), axis=1, keepdims=True)


def _threshold_body(aff_ref, thr_ref, *, cap):
    thr = [jnp.zeros((1, 1), I32) for _ in range(N_EXPERTS)]
    for bit in range(29, -1, -1):
        for e in range(N_EXPERTS):
            bits = pltpu.bitcast(aff_ref[e], I32)
            cand = thr[e] | (1 << bit)
            thr[e] = jnp.where(_count(bits >= cand) >= cap, cand, thr[e])
    for e in range(N_EXPERTS):
        thr_ref[e] = jnp.broadcast_to(thr[e], (1, LANES))


def _select_body(aff_ref, thr_ref, idx_ref, gate_ref, *, cap, tok0):
    a = aff_ref[0]
    segs = a.shape[0]
    bits = pltpu.bitcast(a, I32)
    count = _count
    thr = thr_ref[0][:, 0:1]
    above = bits > thr
    equal = bits == thr
    need = cap - count(above)

    lane_r = lax.broadcasted_iota(I32, (LANES, LANES), 0)
    lane_c = lax.broadcasted_iota(I32, (LANES, LANES), 1)
    upper = (lane_r <= lane_c).astype(BF16)
    seg_r = lax.broadcasted_iota(I32, (segs, segs), 0)
    seg_c = lax.broadcasted_iota(I32, (segs, segs), 1)
    before = (seg_c < seg_r).astype(BF16)

    def prefix(mask):
        incl = jnp.dot(mask.astype(BF16), upper, preferred_element_type=F32)
        tot = jnp.broadcast_to(incl[:, LANES - 1:LANES], (segs, LANES))
        pre = jnp.dot(before, tot.astype(BF16), preferred_element_type=F32)
        return incl, pre, tot

    eq_f = equal.astype(F32)
    incl_e, pre_e, _ = prefix(eq_f)
    chosen = above | (equal & (pre_e + incl_e - eq_f < need))
    incl, pre, tot = prefix(chosen.astype(F32))
    seg_lo = pre[:, 0:1]
    seg_hi = seg_lo + tot[:, 0:1]
    incl_t = jnp.transpose(incl).astype(BF16)
    aff_t = jnp.transpose(a)
    lane_id = lax.broadcasted_iota(I32, (LANES, SLOT_BLOCK), 0).astype(F32)

    for jb in range(cap // SLOT_BLOCK):
        slot = (jb * SLOT_BLOCK + lax.broadcasted_iota(I32, (1, SLOT_BLOCK), 1)).astype(F32)
        seg_of = jnp.sum((seg_hi <= slot).astype(F32), axis=0, keepdims=True)
        onehot = ((seg_lo <= slot) & (slot < seg_hi)).astype(F32)
        rank = slot - jnp.sum(onehot * seg_lo, axis=0, keepdims=True)
        counts = jnp.dot(incl_t, onehot.astype(BF16), preferred_element_type=F32)
        lane_of = jnp.sum((counts <= rank).astype(F32), axis=0, keepdims=True)
        token = (seg_of * LANES + lane_of).astype(I32) + tok0
        idx_ref[0, :, jb * SLOT_BLOCK:(jb + 1) * SLOT_BLOCK] = token * SLABS
        vals = jnp.dot(aff_t, onehot, precision=lax.Precision.HIGHEST, preferred_element_type=F32)
        gate_ref[0, :, jb * SLOT_BLOCK:(jb + 1) * SLOT_BLOCK] = jnp.sum(
            jnp.where(lane_id == lane_of, vals, 0.0), axis=0, keepdims=True)


def _select(aff_group, cap, tok0):
    n = aff_group.shape[1]
    aff3 = aff_group.reshape(N_EXPERTS, n // LANES, LANES)
    thr = pl.pallas_call(
        functools.partial(_threshold_body, cap=cap),
        out_shape=jax.ShapeDtypeStruct((N_EXPERTS, 1, LANES), I32),
        compiler_params=pltpu.CompilerParams(vmem_limit_bytes=VMEM_LIMIT),
        name="expert_threshold",
    )(aff3)
    spec = pl.BlockSpec((1, 1, cap), lambda e: (e, 0, 0))
    return pl.pallas_call(
        functools.partial(_select_body, cap=cap, tok0=tok0),
        grid=(N_EXPERTS,),
        in_specs=[pl.BlockSpec((1, n // LANES, LANES), lambda e: (e, 0, 0)),
                  pl.BlockSpec((1, 1, LANES), lambda e: (e, 0, 0))],
        out_specs=[spec, spec],
        out_shape=[jax.ShapeDtypeStruct((N_EXPERTS, 1, cap), I32), jax.ShapeDtypeStruct((N_EXPERTS, 1, cap), F32)],
        compiler_params=_params("parallel"),
        name="expert_select",
    )(aff3, thr)


def _ffn_body(idx_ref, h_hbm, gate_ref, wg_ref, wu_ref, wd_ref, x_in_hbm, x_hbm,
              hbuf_ref, xbuf_ref, xb_ref, acc_ref, sem_h, sem_x, sem_s, *, rows, n_tiles, steps):
    del x_in_hbm
    g = pl.program_id(0)
    f = pl.program_id(1)
    cur = g % 2
    oth = 1 - cur
    n_early = steps - 3
    per_early = rows // n_early
    base = g * rows
    base_next = jnp.minimum(g + 1, n_tiles - 1) * rows
    base_prev = jnp.maximum(g - 1, 0) * rows

    def hbm_tok(hbm, tile_base, j):
        return hbm.at[pl.ds(pl.multiple_of(idx_ref[tile_base + j], SLABS), SLABS), :]

    def buf_tok(buf, j):
        return buf.at[pl.ds(pl.multiple_of(j * SLABS, SLABS), SLABS), :]

    def h_gather(tile_base, j):
        return pltpu.make_async_copy(hbm_tok(h_hbm, tile_base, j), buf_tok(hbuf_ref, j), sem_h)

    def x_gather(tile_base, j):
        return pltpu.make_async_copy(hbm_tok(x_hbm, tile_base, j), buf_tok(xbuf_ref, j), sem_x)

    def x_scatter(tile_base, j):
        return pltpu.make_async_copy(buf_tok(xbuf_ref, j), hbm_tok(x_hbm, tile_base, j), sem_s)

    def wait_h():
        pltpu.make_async_copy(h_hbm.at[pl.ds(0, SLABS * rows), :], hbuf_ref, sem_h).wait()

    def wait_x():
        pltpu.make_async_copy(x_hbm.at[pl.ds(0, SLABS * rows), :], xbuf_ref, sem_x).wait()

    def wait_s():
        pltpu.make_async_copy(xbuf_ref, x_hbm.at[pl.ds(0, SLABS * rows), :], sem_s).wait()

    def stage_h(slot, row0, n):
        for s in range(SLABS):
            slab = hbuf_ref[pl.ds(row0 * SLABS + s, n, stride=SLABS), :]
            xb_ref[slot, pl.ds(row0, n), s * LANES:(s + 1) * LANES] = slab.astype(BF16)

    def matmuls(first=False):
        xb = xb_ref[cur]
        hg = jnp.dot(xb, wg_ref[0, 0].astype(BF16), preferred_element_type=F32)
        hu = jnp.dot(xb, wu_ref[0, 0].astype(BF16), preferred_element_type=F32)
        hid = (hg * _sigmoid(hg) * hu).astype(BF16)
        y = jnp.dot(hid, wd_ref[0, 0].astype(BF16), preferred_element_type=F32)
        if first:
            acc_ref[...] = y
        else:
            acc_ref[...] += y

    @pl.when((g == 0) & (f == 0))
    def _():
        def issue(j, carry):
            h_gather(0, j).start()
            x_gather(0, j).start()
            return carry
        lax.fori_loop(0, rows, issue, 0)
        wait_h()
        wait_x()
        stage_h(0, 0, rows)

    def early(first):
        j0 = f * per_early
        for i in range(per_early):
            h_gather(base_next, j0 + i).start()
            x_scatter(base_prev, j0 + i).start()
        matmuls(first)

    @pl.when(f == 0)
    def _():
        early(True)

    @pl.when((f > 0) & (f < n_early))
    def _():
        early(False)

    @pl.when(f == n_early)
    def _():
        matmuls()
        wait_s()
        wait_h()

    @pl.when(f == n_early + 1)
    def _():
        for i in range(rows):
            x_gather(base, i).start()
        stage_h(oth, 0, rows)
        matmuls()

    @pl.when(f == steps - 1)
    def _():
        matmuls()
        wait_x()
        gate = gate_ref[0]
        for s in range(SLABS):
            slab = pl.ds(s, rows, stride=SLABS)
            xbuf_ref[slab, :] = xbuf_ref[slab, :] + acc_ref[:, s * LANES:(s + 1) * LANES] * gate

    @pl.when((g == n_tiles - 1) & (f == steps - 1))
    def _():
        def issue(j, carry):
            x_scatter(base, j).start()
            return carry
        lax.fori_loop(0, rows, issue, 0)
        wait_s()


def _expert_ffn(idx_flat, h, gates, w_gate, w_up, w_down, layer, x):
    n_slots = gates.shape[1]
    d_exp = w_gate.shape[3]
    tiles = n_slots // FFN_ROWS
    steps = d_exp // FFN_COLS
    assert steps >= 4 and FFN_ROWS % (steps - 3) == 0 and FFN_ROWS % 2 == 0
    any_spec = pl.BlockSpec(memory_space=pl.ANY)
    grid_spec = pltpu.PrefetchScalarGridSpec(
        num_scalar_prefetch=1,
        grid=(N_EXPERTS * tiles, steps),
        in_specs=[
            any_spec,
            pl.BlockSpec((1, FFN_ROWS, 1), lambda g, f, idx: (g // tiles, g % tiles, 0)),
            pl.BlockSpec((1, 1, D_MODEL, FFN_COLS), lambda g, f, idx: (layer, g // tiles, 0, f)),
            pl.BlockSpec((1, 1, D_MODEL, FFN_COLS), lambda g, f, idx: (layer, g // tiles, 0, f)),
            pl.BlockSpec((1, 1, FFN_COLS, D_MODEL), lambda g, f, idx: (layer, g // tiles, f, 0)),
            any_spec,
        ],
        out_specs=any_spec,
        scratch_shapes=[pltpu.VMEM((SLABS * FFN_ROWS, LANES), F32), pltpu.VMEM((SLABS * FFN_ROWS, LANES), F32),
                        pltpu.VMEM((2, FFN_ROWS, D_MODEL), BF16), pltpu.VMEM((FFN_ROWS, D_MODEL), F32),
                        pltpu.SemaphoreType.DMA, pltpu.SemaphoreType.DMA, pltpu.SemaphoreType.DMA],
    )
    return pl.pallas_call(
        functools.partial(_ffn_body, rows=FFN_ROWS, n_tiles=N_EXPERTS * tiles, steps=steps),
        grid_spec=grid_spec,
        out_shape=jax.ShapeDtypeStruct(x.shape, F32),
        input_output_aliases={6: 0},
        compiler_params=_params("arbitrary", "arbitrary"),
        name="expert_ffn",
    )(idx_flat, h, gates, w_gate, w_up, w_down, x)


def _final_norm_body(x_ref, g_ref, o_ref):
    x = _load_tokens(x_ref, ROW_TILE)
    o_ref[...] = x * lax.rsqrt(jnp.mean(x * x, axis=-1, keepdims=True) + EPS) * g_ref[...]


def _final_norm(xt, g, tok0, n):
    return pl.pallas_call(
        _final_norm_body,
        grid=(n // ROW_TILE,),
        in_specs=[_tiled_spec(ROW_TILE, tok0 // ROW_TILE), pl.BlockSpec(g.shape, lambda i: (0, 0))],
        out_specs=pl.BlockSpec((ROW_TILE, D_MODEL), lambda i: (i, 0)),
        out_shape=jax.ShapeDtypeStruct((n, D_MODEL), F32),
        compiler_params=_params("parallel"),
        name="final_norm",
    )(xt, g)


def _neighbours(seq_lens, size):
    has_prev, has_next = [], []
    for length in seq_lens:
        assert length % size == 0
        k = length // size
        has_prev += [int(i > 0) for i in range(k)]
        has_next += [int(i < k - 1) for i in range(k)]
    return jnp.asarray(has_prev, I32), jnp.asarray(has_next, I32)


def _trunk(x, groups, norm_mix_g, w_in, conv_w, conv_b, conv_ln_g, conv_ln_b, w_out, rel_bias,
           norm_ffn_g, w_router, w_gate, w_up, w_down, norm_final_g):
    seq_lens = [s for b, s in groups for _ in range(b)]
    has_prev, has_next = _neighbours(seq_lens, CHUNK)
    tile_prev, tile_next = _neighbours(seq_lens, ROW_TILE)
    tables = _bias_tables(rel_bias)
    depth = w_in.shape[0]
    row2 = lambda a: a.reshape(1, -1)
    for l in range(depth):
        q, k, v, conv_out = _mixer_in(x, tile_prev, tile_next, row2(norm_mix_g[l]), w_in[l].astype(BF16),
                                      conv_w[l], row2(conv_b[l]), row2(conv_ln_g[l]), row2(conv_ln_b[l]))
        attn_out = _dilated_attention(q, k, v, has_prev, has_next, tables)
        wr = jnp.pad(w_router[l], ((0, 0), (0, LANES - N_EXPERTS)))
        wr_hi = wr.astype(BF16)
        wr_lo = (wr - wr_hi.astype(F32)).astype(BF16)
        x, h, aff = _mixer_out(conv_out, attn_out, x, w_out[l].astype(BF16), row2(norm_ffn_g[l]),
                               jnp.concatenate([wr_hi, wr_lo], axis=1))
        idx, gates, tok0 = [], [], 0
        for b, s in groups:
            n = b * s
            cap = CAPACITY_FACTOR * n // N_EXPERTS
            i, g = _select(aff[:, tok0:tok0 + n], cap, tok0)
            idx.append(i[:, 0, :])
            gates.append(g[:, 0, :])
            tok0 += n
        idx = jnp.concatenate(idx, axis=1)
        gates = jnp.concatenate(gates, axis=1)[:, :, None]
        x = _expert_ffn(idx.reshape(-1), h, gates, w_gate, w_up, w_down, l, x)
    outs, tok0 = [], 0
    for b, s in groups:
        outs.append(_final_norm(x, row2(norm_final_g), tok0, b * s).reshape(b, s, D_MODEL))
        tok0 += b * s
    return outs


def kernel(x_prompt, x_sample, norm_mix_g, w_in, conv_w, conv_b, conv_ln_g, conv_ln_b, w_out, rel_bias,
           norm_ffn_g, w_router, w_gate, w_up, w_down, norm_final_g):
    groups = [x_prompt.shape[:2], x_sample.shape[:2]]
    x = _row_tiled_concat(x_prompt.reshape(-1, D_MODEL), x_sample.reshape(-1, D_MODEL))
    y_prompt, y_sample = _trunk(x, groups, norm_mix_g, w_in, conv_w, conv_b, conv_ln_g, conv_ln_b, w_out,
                                rel_bias, norm_ffn_g, w_router, w_gate, w_up, w_down, norm_final_g)
    return y_prompt, y_sample
```

```python
import functools

import jax
import jax.numpy as jnp
import numpy as np
from jax import lax
from jax.experimental import pallas as pl
from jax.experimental.pallas import tpu as pltpu

F32 = jnp.float32
BF16 = jnp.bfloat16
I32 = jnp.int32

D_MODEL = 1024
D_CONV = 512
N_HEADS = 8
HEAD_DIM = 64
D_ATTN = N_HEADS * HEAD_DIM
CONV_WIDTH = 31
ATTN_DILATIONS = (1, 4, 16)
ATTN_HALF = 64
N_BUCKETS = 32
MAX_DISTANCE = 1024
N_EXPERTS = 16
CAPACITY_FACTOR = 2
EPS = 1e-6
NEG = -1e30
LOG2E = 1.4426950408889634

LANES = 128
QGROUP = 128
KGROUP = QGROUP + 2 * ATTN_HALF
CHUNK = QGROUP * max(ATTN_DILATIONS)
HALO = ATTN_HALF * max(ATTN_DILATIONS)
CONV_PAD = 16
ROW_TILE = 512
FFN_ROWS = 1024
FFN_COLS = 256
SLOT_BLOCK = 512
VMEM_LIMIT = 56 * 1024 * 1024


def _params(*sem):
    return pltpu.CompilerParams(dimension_semantics=sem, vmem_limit_bytes=VMEM_LIMIT)


def _sigmoid(x):
    return 1.0 / (1.0 + jnp.exp(-x))


SLABS = D_MODEL // LANES


def _load_tokens(ref, n):
    return jnp.concatenate([ref[pl.ds(s, n, stride=SLABS), :] for s in range(SLABS)], axis=1)


def _store_tokens(ref, val):
    n = val.shape[0]
    for s in range(SLABS):
        ref[pl.ds(s, n, stride=SLABS), :] = val[:, s * LANES:(s + 1) * LANES]


def _tiled_spec(rows, first=0):
    return pl.BlockSpec((SLABS * rows, LANES), lambda i, *_: (i + first, 0))


def _row_tiled_body(xa_ref, xb_ref, o_ref, *, tiles_a):
    i = pl.program_id(0)

    @pl.when(i < tiles_a)
    def _():
        _store_tokens(o_ref, xa_ref[...])

    @pl.when(i >= tiles_a)
    def _():
        _store_tokens(o_ref, xb_ref[...])


def _row_tiled_concat(xa, xb):
    tiles_a = xa.shape[0] // ROW_TILE
    tiles_b = xb.shape[0] // ROW_TILE
    n = xa.shape[0] + xb.shape[0]
    return pl.pallas_call(
        functools.partial(_row_tiled_body, tiles_a=tiles_a),
        grid=(tiles_a + tiles_b,),
        in_specs=[pl.BlockSpec((ROW_TILE, D_MODEL), lambda i: (jnp.minimum(i, tiles_a - 1), 0)),
                  pl.BlockSpec((ROW_TILE, D_MODEL), lambda i: (jnp.maximum(i - tiles_a, 0), 0))],
        out_specs=_tiled_spec(ROW_TILE),
        out_shape=jax.ShapeDtypeStruct((SLABS * n, LANES), F32),
        compiler_params=_params("parallel"),
        name="row_tiled_concat",
    )(xa, xb)


CONV_ROWS = 64
CONV_LANES = 256


def _mixer_in_body(hp_ref, hn_ref, x_ref, g_ref, w_ref, cw_ref, cb_ref, lg_ref, lb_ref,
                   q_ref, k_ref, v_ref, c_ref, win_ref, u_ref, *, n_tiles):
    i = pl.program_id(0)
    prev_tile = jnp.maximum(i - 1, 0)
    this_tile = jnp.minimum(i, n_tiles - 1)

    @pl.when(i == 0)
    def _():
        win_ref[...] = jnp.zeros_like(win_ref)

    x = _load_tokens(x_ref, ROW_TILE)
    xn = (x * lax.rsqrt(jnp.mean(x * x, axis=-1, keepdims=True) + EPS) * g_ref[...]).astype(BF16)

    def proj(j, width):
        return jnp.dot(xn, w_ref[:, j:j + width], preferred_element_type=F32)

    val = proj(0, D_CONV)
    gate = proj(D_CONV, D_CONV)
    u_ref[...] = val * _sigmoid(gate)
    q_ref[...] = proj(2 * D_CONV, D_ATTN) * (HEAD_DIM ** -0.5 * LOG2E)
    k_ref[...] = proj(2 * D_CONV + D_ATTN, D_ATTN)
    v_ref[...] = proj(2 * D_CONV + 2 * D_ATTN, D_ATTN)

    body = CONV_PAD + ROW_TILE
    win_ref[body:, :] = u_ref[0:CONV_PAD, :] * (hn_ref[prev_tile] > 0).astype(F32)
    first = CONV_PAD - CONV_WIDTH // 2
    span = CONV_ROWS + 2 * CONV_PAD
    for r0 in range(0, ROW_TILE, CONV_ROWS):
        halves = []
        for c0 in range(0, D_CONV, CONV_LANES):
            win = win_ref[r0:r0 + span, c0:c0 + CONV_LANES]
            acc = jnp.zeros((CONV_ROWS, CONV_LANES), F32)
            for phase in range(8):
                taps = [k for k in range(CONV_WIDTH) if (first + k) % 8 == phase]
                if not taps:
                    continue
                shifted = win if phase == 0 else pltpu.roll(win, span - phase, axis=0)
                for k in taps:
                    off = first + k - phase
                    acc = acc + shifted[off:off + CONV_ROWS, :] * cw_ref[k:k + 1, c0:c0 + CONV_LANES]
            halves.append(acc)
        y = jnp.concatenate(halves, axis=1) + cb_ref[...]
        mu = jnp.mean(y, axis=-1, keepdims=True)
        yc = y - mu
        var = jnp.mean(yc * yc, axis=-1, keepdims=True)
        yn = yc * lax.rsqrt(var + EPS) * lg_ref[...] + lb_ref[...]
        c_ref[r0:r0 + CONV_ROWS, :] = (yn * _sigmoid(yn)).astype(BF16)

    win_ref[0:CONV_PAD, :] = win_ref[ROW_TILE:body, :] * (hp_ref[this_tile] > 0).astype(F32)
    win_ref[CONV_PAD:body, :] = u_ref[...]


def _mixer_in(xt, tile_prev, tile_next, g, w_in_bf16, conv_w, conv_b, ln_g, ln_b):
    n = xt.shape[0] // SLABS
    n_tiles = n // ROW_TILE
    cur = lambda i, hp, hn: (jnp.minimum(i, n_tiles - 1), 0)
    full = lambda a: pl.BlockSpec(a.shape, lambda i, hp, hn: (0,) * a.ndim)
    row = pl.BlockSpec((ROW_TILE, D_ATTN), cur)
    grid_spec = pltpu.PrefetchScalarGridSpec(
        num_scalar_prefetch=2,
        grid=(n_tiles + 1,),
        in_specs=[pl.BlockSpec((SLABS * ROW_TILE, LANES), cur), full(g), full(w_in_bf16),
                  full(conv_w), full(conv_b), full(ln_g), full(ln_b)],
        out_specs=[row, row, row,
                   pl.BlockSpec((ROW_TILE, D_CONV), lambda i, hp, hn: (jnp.maximum(i - 1, 0), 0))],
        scratch_shapes=[pltpu.VMEM((ROW_TILE + 2 * CONV_PAD, D_CONV), F32), pltpu.VMEM((ROW_TILE, D_CONV), F32)],
    )
    return pl.pallas_call(
        functools.partial(_mixer_in_body, n_tiles=n_tiles),
        grid_spec=grid_spec,
        out_shape=[jax.ShapeDtypeStruct((n, D_ATTN), F32)] * 3 + [jax.ShapeDtypeStruct((n, D_CONV), BF16)],
        compiler_params=_params("arbitrary"),
        name="mixer_in_conv",
    )(tile_prev, tile_next, xt, g, w_in_bf16, conv_w, conv_b, ln_g, ln_b)


def _t5_bucket(rel):
    half = N_BUCKETS // 2
    max_exact = half // 2
    ret = np.where(rel > 0, half, 0)
    n = np.abs(rel)
    large = max_exact + (np.log(np.maximum(n, 1) / max_exact)
                         / np.log(MAX_DISTANCE / max_exact) * (half - max_exact)).astype(np.int32)
    large = np.minimum(large, half - 1)
    return (ret + np.where(n < max_exact, n, large)).astype(np.int32)


def _bias_tables(rel_bias):
    delta = np.arange(QGROUP + KGROUP - 1) - (QGROUP - 1) - ATTN_HALF
    band = np.abs(delta) <= ATTN_HALF
    tabs = []
    for d in ATTN_DILATIONS:
        onehot = np.eye(N_BUCKETS, dtype=np.float32)[_t5_bucket(d * delta)]
        base = jnp.dot(onehot, rel_bias.astype(F32), precision=lax.Precision.HIGHEST)
        base = jnp.where(band[:, None], base * LOG2E, NEG)
        rows = [base[QGROUP - 1 - i:QGROUP - 1 - i + KGROUP] for i in range(QGROUP)]
        b = jnp.transpose(jnp.stack(rows, axis=0), (2, 0, 1))
        tabs.append(b.reshape(N_HEADS // 2, 2 * QGROUP, KGROUP))
    return jnp.stack(tabs, axis=0)


MERGE_ROWS = 256


def _attn_body(hp_ref, hn_ref, q_ref, kp_ref, kc_ref, kn_ref, vp_ref, vc_ref, vn_ref, tab_ref, o_ref,
               kw_ref, vw_ref, *part_refs):
    nb = len(ATTN_DILATIONS)
    ob_refs, mb_refs, lb_refs = part_refs[:nb], part_refs[nb:2 * nb], part_refs[2 * nb:]
    c = pl.program_id(0)
    has_prev = hp_ref[c] > 0
    has_next = hn_ref[c] > 0
    kw_ref[0:HALO, :] = kp_ref[...]
    kw_ref[HALO:HALO + CHUNK, :] = kc_ref[...]
    kw_ref[HALO + CHUNK:, :] = kn_ref[...]
    vw_ref[0:HALO, :] = vp_ref[...]
    vw_ref[HALO:HALO + CHUNK, :] = vc_ref[...]
    vw_ref[HALO + CHUNK:, :] = vn_ref[...]

    first_head = lax.broadcasted_iota(I32, (QGROUP, LANES), 1) < HEAD_DIM
    key_pos = lax.broadcasted_iota(I32, (1, KGROUP), 1)

    for bi, d in enumerate(ATTN_DILATIONS):
        stride = None if d == 1 else d
        for g in range(CHUNK // QGROUP):
            res, blk = g % d, g // d
            q0 = blk * (QGROUP * d) + res
            k0 = HALO + (blk * QGROUP - ATTN_HALF) * d + res
            k_last = k0 + d * (KGROUP - 1)
            q = q_ref[pl.ds(q0, QGROUP, stride=stride), :]
            k = kw_ref[pl.ds(k0, KGROUP, stride=stride), :]
            v = vw_ref[pl.ds(k0, KGROUP, stride=stride), :]
            q2 = jnp.concatenate([jnp.where(first_head, q, 0.0), jnp.where(first_head, 0.0, q)], axis=0)
            s = lax.dot_general(q2.astype(BF16), k.astype(BF16), (((1,), (1,)), ((), ())),
                                preferred_element_type=F32)
            s = s + tab_ref[bi, 0]
            row = k0 + d * key_pos
            if k0 < HALO:
                s = jnp.where((row >= HALO) | has_prev, s, NEG)
            if k_last >= HALO + CHUNK:
                s = jnp.where((row < HALO + CHUNK) | has_next, s, NEG)
            m = jnp.max(s, axis=-1, keepdims=True)
            p = jnp.exp2(s - m)
            l = jnp.sum(p, axis=-1, keepdims=True)
            o = jnp.dot(p.astype(BF16), v.astype(BF16), preferred_element_type=F32)
            rows = pl.ds(q0, QGROUP, stride=stride)
            ob_refs[bi][rows, :] = jnp.where(first_head, o[:QGROUP], o[QGROUP:])
            mb_refs[bi][rows, :] = jnp.where(first_head, m[:QGROUP], m[QGROUP:])
            lb_refs[bi][rows, :] = jnp.where(first_head, l[:QGROUP], l[QGROUP:])

    def merge(t, carry):
        rows = pl.ds(pl.multiple_of(t * MERGE_ROWS, MERGE_ROWS), MERGE_ROWS)
        ms = [r[rows, :] for r in mb_refs]
        m_tot = functools.reduce(jnp.maximum, ms)
        ws = [jnp.exp2(m - m_tot) for m in ms]
        num = sum(w * r[rows, :] for w, r in zip(ws, ob_refs))
        den = sum(w * r[rows, :] for w, r in zip(ws, lb_refs))
        o_ref[rows, :] = (num / den).astype(BF16)
        return carry

    lax.fori_loop(0, CHUNK // MERGE_ROWS, merge, 0)


def _dilated_attention(q, k, v, has_prev, has_next, tables):
    n = q.shape[0]
    n_chunks = n // CHUNK
    per = CHUNK // HALO
    last = n // HALO - 1
    cur = pl.BlockSpec((CHUNK, LANES), lambda c, h, hp, hn: (c, h))
    prev = pl.BlockSpec((HALO, LANES), lambda c, h, hp, hn: (jnp.maximum(c * per - 1, 0), h))
    nxt = pl.BlockSpec((HALO, LANES), lambda c, h, hp, hn: (jnp.minimum((c + 1) * per, last), h))
    grid_spec = pltpu.PrefetchScalarGridSpec(
        num_scalar_prefetch=2,
        grid=(n_chunks, D_ATTN // LANES),
        in_specs=[cur, prev, cur, nxt, prev, cur, nxt,
                  pl.BlockSpec((len(ATTN_DILATIONS), 1, 2 * QGROUP, KGROUP), lambda c, h, hp, hn: (0, h, 0, 0))],
        out_specs=cur,
        scratch_shapes=([pltpu.VMEM((CHUNK + 2 * HALO, LANES), F32)] * 2
                        + [pltpu.VMEM((CHUNK, LANES), F32)] * (3 * len(ATTN_DILATIONS))),
    )
    return pl.pallas_call(
        _attn_body,
        grid_spec=grid_spec,
        out_shape=jax.ShapeDtypeStruct((n, D_ATTN), BF16),
        compiler_params=_params("parallel", "parallel"),
        name="dilated_attention",
    )(has_prev, has_next, q, k, k, k, v, v, v, tables)


def _mixer_out_body(cv_ref, at_ref, x_ref, w_ref, g_ref, wr_ref, xo_ref, h_ref, aff_ref):
    y = (jnp.dot(cv_ref[...], w_ref[0:D_CONV, :], preferred_element_type=F32)
         + jnp.dot(at_ref[...], w_ref[D_CONV:, :], preferred_element_type=F32))
    x = _load_tokens(x_ref, ROW_TILE) + y
    _store_tokens(xo_ref, x)
    h = x * lax.rsqrt(jnp.mean(x * x, axis=-1, keepdims=True) + EPS) * g_ref[...]
    _store_tokens(h_ref, h)
    h_hi = h.astype(BF16)
    h_lo = (h - h_hi.astype(F32)).astype(BF16)
    both = jnp.dot(h_hi, wr_ref[...], preferred_element_type=F32)
    low = jnp.dot(h_lo, wr_ref[:, :LANES], preferred_element_type=F32)
    logits = both[:, :LANES] + both[:, LANES:] + low
    logits = jnp.transpose(logits)[:N_EXPERTS, :]
    e = jnp.exp(logits - jnp.max(logits, axis=0, keepdims=True))
    aff_ref[...] = e / jnp.sum(e, axis=0, keepdims=True)


def _mixer_out(conv_out, attn_out, xt, w_out_bf16, g, w_router_pad):
    n = xt.shape[0] // SLABS
    row = lambda w: pl.BlockSpec((ROW_TILE, w), lambda i: (i, 0))
    full = lambda a: pl.BlockSpec(a.shape, lambda i: (0,) * a.ndim)
    tiled = jax.ShapeDtypeStruct(xt.shape, F32)
    return pl.pallas_call(
        _mixer_out_body,
        grid=(n // ROW_TILE,),
        in_specs=[row(D_CONV), row(D_ATTN), _tiled_spec(ROW_TILE), full(w_out_bf16), full(g), full(w_router_pad)],
        out_specs=[_tiled_spec(ROW_TILE), _tiled_spec(ROW_TILE),
                   pl.BlockSpec((N_EXPERTS, ROW_TILE), lambda i: (0, i))],
        out_shape=[tiled, tiled, jax.ShapeDtypeStruct((N_EXPERTS, n), F32)],
        input_output_aliases={2: 0},
        compiler_params=_params("parallel"),
        name="mixer_out_router",
    )(conv_out, attn_out, xt, w_out_bf16, g, w_router_pad)


def _count(mask):
    return jnp.sum(jnp.sum(mask.astype(F32), axis=0, keepdims=True), axis=1, keepdims=True)


def _threshold_body(aff_ref, thr_ref, *, cap):
    thr = [jnp.zeros((1, 1), I32) for _ in range(N_EXPERTS)]
    for bit in range(29, -1, -1):
        for e in range(N_EXPERTS):
            bits = pltpu.bitcast(aff_ref[e], I32)
            cand = thr[e] | (1 << bit)
            thr[e] = jnp.where(_count(bits >= cand) >= cap, cand, thr[e])
    for e in range(N_EXPERTS):
        thr_ref[e] = jnp.broadcast_to(thr[e], (1, LANES))


def _select_body(aff_ref, thr_ref, idx_ref, gate_ref, *, cap, tok0):
    a = aff_ref[0]
    segs = a.shape[0]
    bits = pltpu.bitcast(a, I32)
    count = _count
    thr = thr_ref[0][:, 0:1]
    above = bits > thr
    equal = bits == thr
    need = cap - count(above)

    lane_r = lax.broadcasted_iota(I32, (LANES, LANES), 0)
    lane_c = lax.broadcasted_iota(I32, (LANES, LANES), 1)
    upper = (lane_r <= lane_c).astype(BF16)
    seg_r = lax.broadcasted_iota(I32, (segs, segs), 0)
    seg_c = lax.broadcasted_iota(I32, (segs, segs), 1)
    before = (seg_c < seg_r).astype(BF16)

    def prefix(mask):
        incl = jnp.dot(mask.astype(BF16), upper, preferred_element_type=F32)
        tot = jnp.broadcast_to(incl[:, LANES - 1:LANES], (segs, LANES))
        pre = jnp.dot(before, tot.astype(BF16), preferred_element_type=F32)
        return incl, pre, tot

    eq_f = equal.astype(F32)
    incl_e, pre_e, _ = prefix(eq_f)
    chosen = above | (equal & (pre_e + incl_e - eq_f < need))
    incl, pre, tot = prefix(chosen.astype(F32))
    seg_lo = pre[:, 0:1]
    seg_hi = seg_lo + tot[:, 0:1]
    incl_t = jnp.transpose(incl).astype(BF16)
    aff_t = jnp.transpose(a)
    lane_id = lax.broadcasted_iota(I32, (LANES, SLOT_BLOCK), 0).astype(F32)

    for jb in range(cap // SLOT_BLOCK):
        slot = (jb * SLOT_BLOCK + lax.broadcasted_iota(I32, (1, SLOT_BLOCK), 1)).astype(F32)
        seg_of = jnp.sum((seg_hi <= slot).astype(F32), axis=0, keepdims=True)
        onehot = ((seg_lo <= slot) & (slot < seg_hi)).astype(F32)
        rank = slot - jnp.sum(onehot * seg_lo, axis=0, keepdims=True)
        counts = jnp.dot(incl_t, onehot.astype(BF16), preferred_element_type=F32)
        lane_of = jnp.sum((counts <= rank).astype(F32), axis=0, keepdims=True)
        token = (seg_of * LANES + lane_of).astype(I32) + tok0
        idx_ref[0, :, jb * SLOT_BLOCK:(jb + 1) * SLOT_BLOCK] = token * SLABS
        vals = jnp.dot(aff_t, onehot, precision=lax.Precision.HIGHEST, preferred_element_type=F32)
        gate_ref[0, :, jb * SLOT_BLOCK:(jb + 1) * SLOT_BLOCK] = jnp.sum(
            jnp.where(lane_id == lane_of, vals, 0.0), axis=0, keepdims=True)


def _select(aff_group, cap, tok0):
    n = aff_group.shape[1]
    aff3 = aff_group.reshape(N_EXPERTS, n // LANES, LANES)
    thr = pl.pallas_call(
        functools.partial(_threshold_body, cap=cap),
        out_shape=jax.ShapeDtypeStruct((N_EXPERTS, 1, LANES), I32),
        compiler_params=pltpu.CompilerParams(vmem_limit_bytes=VMEM_LIMIT),
        name="expert_threshold",
    )(aff3)
    spec = pl.BlockSpec((1, 1, cap), lambda e: (e, 0, 0))
    return pl.pallas_call(
        functools.partial(_select_body, cap=cap, tok0=tok0),
        grid=(N_EXPERTS,),
        in_specs=[pl.BlockSpec((1, n // LANES, LANES), lambda e: (e, 0, 0)),
                  pl.BlockSpec((1, 1, LANES), lambda e: (e, 0, 0))],
        out_specs=[spec, spec],
        out_shape=[jax.ShapeDtypeStruct((N_EXPERTS, 1, cap), I32), jax.ShapeDtypeStruct((N_EXPERTS, 1, cap), F32)],
        compiler_params=_params("parallel"),
        name="expert_select",
    )(aff3, thr)


WEIGHT_DMA_PRIORITY = 1


def _ffn_body(idx_ref, h_hbm, gate_ref, wg_hbm, wu_hbm, wd_hbm, x_in_hbm, x_hbm,
              hbuf_ref, xbuf_ref, xb_ref, acc_ref, wg_ref, wu_ref, wd_ref, sem_h, sem_x, sem_s, sem_w,
              *, rows, n_tiles, steps, tiles, layer):
    del x_in_hbm
    g = pl.program_id(0)
    f = pl.program_id(1)
    cur = g % 2
    oth = 1 - cur
    n_early = steps - 3
    per_early = rows // n_early
    base = g * rows
    base_next = jnp.minimum(g + 1, n_tiles - 1) * rows
    base_prev = jnp.maximum(g - 1, 0) * rows

    def hbm_tok(hbm, tile_base, j):
        return hbm.at[pl.ds(pl.multiple_of(idx_ref[tile_base + j], SLABS), SLABS), :]

    def buf_tok(buf, j):
        return buf.at[pl.ds(pl.multiple_of(j * SLABS, SLABS), SLABS), :]

    def h_gather(tile_base, j):
        return pltpu.make_async_copy(hbm_tok(h_hbm, tile_base, j), buf_tok(hbuf_ref, j), sem_h)

    def x_gather(tile_base, j):
        return pltpu.make_async_copy(hbm_tok(x_hbm, tile_base, j), buf_tok(xbuf_ref, j), sem_x)

    def x_scatter(tile_base, j):
        return pltpu.make_async_copy(buf_tok(xbuf_ref, j), hbm_tok(x_hbm, tile_base, j), sem_s)

    def wait_h():
        pltpu.make_async_copy(h_hbm.at[pl.ds(0, SLABS * rows), :], hbuf_ref, sem_h).wait()

    def wait_x():
        pltpu.make_async_copy(x_hbm.at[pl.ds(0, SLABS * rows), :], xbuf_ref, sem_x).wait()

    def wait_s():
        pltpu.make_async_copy(xbuf_ref, x_hbm.at[pl.ds(0, SLABS * rows), :], sem_s).wait()

    def stage_h(slot, row0, n):
        for s in range(SLABS):
            slab = hbuf_ref[pl.ds(row0 * SLABS + s, n, stride=SLABS), :]
            xb_ref[slot, pl.ds(row0, n), s * LANES:(s + 1) * LANES] = slab.astype(BF16)

    t = g * steps + f
    wslot = t % 2

    def weight_copies(step, slot):
        e = (step // steps) // tiles
        c0 = pl.multiple_of((step % steps) * FFN_COLS, FFN_COLS)
        return (pltpu.make_async_copy(wg_hbm.at[layer, e, :, pl.ds(c0, FFN_COLS)], wg_ref.at[slot], sem_w.at[slot]),
                pltpu.make_async_copy(wu_hbm.at[layer, e, :, pl.ds(c0, FFN_COLS)], wu_ref.at[slot], sem_w.at[slot]),
                pltpu.make_async_copy(wd_hbm.at[layer, e, pl.ds(c0, FFN_COLS), :], wd_ref.at[slot], sem_w.at[slot]))

    @pl.when(t == 0)
    def _():
        for copy in weight_copies(t, wslot):
            copy.start(priority=WEIGHT_DMA_PRIORITY)

    for copy in weight_copies(t, wslot):
        copy.wait()

    @pl.when(t + 1 < n_tiles * steps)
    def _():
        for copy in weight_copies(t + 1, 1 - wslot):
            copy.start(priority=WEIGHT_DMA_PRIORITY)

    def matmuls(first=False):
        xb = xb_ref[cur]
        hg = jnp.dot(xb, wg_ref[wslot].astype(BF16), preferred_element_type=F32)
        hu = jnp.dot(xb, wu_ref[wslot].astype(BF16), preferred_element_type=F32)
        hid = (hg * _sigmoid(hg) * hu).astype(BF16)
        y = jnp.dot(hid, wd_ref[wslot].astype(BF16), preferred_element_type=F32)
        if first:
            acc_ref[...] = y
        else:
            acc_ref[...] += y

    @pl.when((g == 0) & (f == 0))
    def _():
        def issue(j, carry):
            h_gather(0, j).start()
            x_gather(0, j).start()
            return carry
        lax.fori_loop(0, rows, issue, 0)
        wait_h()
        wait_x()
        stage_h(0, 0, rows)

    def early(first):
        j0 = f * per_early
        for i in range(per_early):
            h_gather(base_next, j0 + i).start()
            x_scatter(base_prev, j0 + i).start()
        matmuls(first)

    @pl.when(f == 0)
    def _():
        early(True)

    @pl.when((f > 0) & (f < n_early))
    def _():
        early(False)

    @pl.when(f == n_early)
    def _():
        matmuls()
        wait_s()
        wait_h()

    @pl.when(f == n_early + 1)
    def _():
        for i in range(rows):
            x_gather(base, i).start()
        stage_h(oth, 0, rows)
        matmuls()

    @pl.when(f == steps - 1)
    def _():
        matmuls()
        wait_x()
        gate = gate_ref[0]
        for s in range(SLABS):
            slab = pl.ds(s, rows, stride=SLABS)
            xbuf_ref[slab, :] = xbuf_ref[slab, :] + acc_ref[:, s * LANES:(s + 1) * LANES] * gate

    @pl.when((g == n_tiles - 1) & (f == steps - 1))
    def _():
        def issue(j, carry):
            x_scatter(base, j).start()
            return carry
        lax.fori_loop(0, rows, issue, 0)
        wait_s()


def _expert_ffn(idx_flat, h, gates, w_gate, w_up, w_down, layer, x):
    n_slots = gates.shape[1]
    d_exp = w_gate.shape[3]
    tiles = n_slots // FFN_ROWS
    steps = d_exp // FFN_COLS
    assert steps >= 4 and FFN_ROWS % (steps - 3) == 0 and FFN_ROWS % 2 == 0
    any_spec = pl.BlockSpec(memory_space=pl.ANY)
    grid_spec = pltpu.PrefetchScalarGridSpec(
        num_scalar_prefetch=1,
        grid=(N_EXPERTS * tiles, steps),
        in_specs=[
            any_spec,
            pl.BlockSpec((1, FFN_ROWS, 1), lambda g, f, idx: (g // tiles, g % tiles, 0)),
            any_spec, any_spec, any_spec,
            any_spec,
        ],
        out_specs=any_spec,
        scratch_shapes=[pltpu.VMEM((SLABS * FFN_ROWS, LANES), F32), pltpu.VMEM((SLABS * FFN_ROWS, LANES), F32),
                        pltpu.VMEM((2, FFN_ROWS, D_MODEL), BF16), pltpu.VMEM((FFN_ROWS, D_MODEL), F32),
                        pltpu.VMEM((2, D_MODEL, FFN_COLS), F32), pltpu.VMEM((2, D_MODEL, FFN_COLS), F32),
                        pltpu.VMEM((2, FFN_COLS, D_MODEL), F32),
                        pltpu.SemaphoreType.DMA, pltpu.SemaphoreType.DMA, pltpu.SemaphoreType.DMA,
                        pltpu.SemaphoreType.DMA((2,))],
    )
    return pl.pallas_call(
        functools.partial(_ffn_body, rows=FFN_ROWS, n_tiles=N_EXPERTS * tiles, steps=steps, tiles=tiles,
                          layer=layer),
        grid_spec=grid_spec,
        out_shape=jax.ShapeDtypeStruct(x.shape, F32),
        input_output_aliases={6: 0},
        compiler_params=_params("arbitrary", "arbitrary"),
        name="expert_ffn",
    )(idx_flat, h, gates, w_gate, w_up, w_down, x)


def _final_norm_body(x_ref, g_ref, o_ref):
    x = _load_tokens(x_ref, ROW_TILE)
    o_ref[...] = x * lax.rsqrt(jnp.mean(x * x, axis=-1, keepdims=True) + EPS) * g_ref[...]


def _final_norm(xt, g, tok0, n):
    return pl.pallas_call(
        _final_norm_body,
        grid=(n // ROW_TILE,),
        in_specs=[_tiled_spec(ROW_TILE, tok0 // ROW_TILE), pl.BlockSpec(g.shape, lambda i: (0, 0))],
        out_specs=pl.BlockSpec((ROW_TILE, D_MODEL), lambda i: (i, 0)),
        out_shape=jax.ShapeDtypeStruct((n, D_MODEL), F32),
        compiler_params=_params("parallel"),
        name="final_norm",
    )(xt, g)


def _neighbours(seq_lens, size):
    has_prev, has_next = [], []
    for length in seq_lens:
        assert length % size == 0
        k = length // size
        has_prev += [int(i > 0) for i in range(k)]
        has_next += [int(i < k - 1) for i in range(k)]
    return jnp.asarray(has_prev, I32), jnp.asarray(has_next, I32)


def _trunk(x, groups, norm_mix_g, w_in, conv_w, conv_b, conv_ln_g, conv_ln_b, w_out, rel_bias,
           norm_ffn_g, w_router, w_gate, w_up, w_down, norm_final_g):
    seq_lens = [s for b, s in groups for _ in range(b)]
    has_prev, has_next = _neighbours(seq_lens, CHUNK)
    tile_prev, tile_next = _neighbours(seq_lens, ROW_TILE)
    tables = _bias_tables(rel_bias)
    depth = w_in.shape[0]
    row2 = lambda a: a.reshape(1, -1)
    for l in range(depth):
        q, k, v, conv_out = _mixer_in(x, tile_prev, tile_next, row2(norm_mix_g[l]), w_in[l].astype(BF16),
                                      conv_w[l], row2(conv_b[l]), row2(conv_ln_g[l]), row2(conv_ln_b[l]))
        attn_out = _dilated_attention(q, k, v, has_prev, has_next, tables)
        wr = jnp.pad(w_router[l], ((0, 0), (0, LANES - N_EXPERTS)))
        wr_hi = wr.astype(BF16)
        wr_lo = (wr - wr_hi.astype(F32)).astype(BF16)
        x, h, aff = _mixer_out(conv_out, attn_out, x, w_out[l].astype(BF16), row2(norm_ffn_g[l]),
                               jnp.concatenate([wr_hi, wr_lo], axis=1))
        idx, gates, tok0 = [], [], 0
        for b, s in groups:
            n = b * s
            cap = CAPACITY_FACTOR * n // N_EXPERTS
            i, g = _select(aff[:, tok0:tok0 + n], cap, tok0)
            idx.append(i[:, 0, :])
            gates.append(g[:, 0, :])
            tok0 += n
        idx = jnp.concatenate(idx, axis=1)
        gates = jnp.concatenate(gates, axis=1)[:, :, None]
        x = _expert_ffn(idx.reshape(-1), h, gates, w_gate, w_up, w_down, l, x)
    outs, tok0 = [], 0
    for b, s in groups:
        outs.append(_final_norm(x, row2(norm_final_g), tok0, b * s).reshape(b, s, D_MODEL))
        tok0 += b * s
    return outs


def kernel(x_prompt, x_sample, norm_mix_g, w_in, conv_w, conv_b, conv_ln_g, conv_ln_b, w_out, rel_bias,
           norm_ffn_g, w_router, w_gate, w_up, w_down, norm_final_g):
    groups = [x_prompt.shape[:2], x_sample.shape[:2]]
    x = _row_tiled_concat(x_prompt.reshape(-1, D_MODEL), x_sample.reshape(-1, D_MODEL))
    y_prompt, y_sample = _trunk(x, groups, norm_mix_g, w_in, conv_w, conv_b, conv_ln_g, conv_ln_b, w_out,
                                rel_bias, norm_ffn_g, w_router, w_gate, w_up, w_down, norm_final_g)
    return y_prompt, y_sample
```

```python
import functools

import jax
import jax.numpy as jnp
import numpy as np
from jax import lax
from jax.experimental import pallas as pl
from jax.experimental.pallas import tpu as pltpu

F32 = jnp.float32
BF16 = jnp.bfloat16
I32 = jnp.int32

D_MODEL = 1024
D_CONV = 512
N_HEADS = 8
HEAD_DIM = 64
D_ATTN = N_HEADS * HEAD_DIM
CONV_WIDTH = 31
ATTN_DILATIONS = (1, 4, 16)
ATTN_HALF = 64
N_BUCKETS = 32
MAX_DISTANCE = 1024
N_EXPERTS = 16
CAPACITY_FACTOR = 2
EPS = 1e-6
NEG = -1e30
LOG2E = 1.4426950408889634

LANES = 128
QGROUP = 128
KGROUP = QGROUP + 2 * ATTN_HALF
CHUNK = QGROUP * max(ATTN_DILATIONS)
HALO = ATTN_HALF * max(ATTN_DILATIONS)
CONV_PAD = 16
ROW_TILE = 512
FFN_ROWS = 1024
FFN_COLS = 256
SLOT_BLOCK = 512
VMEM_LIMIT = 56 * 1024 * 1024


def _params(*sem):
    return pltpu.CompilerParams(dimension_semantics=sem, vmem_limit_bytes=VMEM_LIMIT)


def _sigmoid(x):
    return 1.0 / (1.0 + jnp.exp(-x))


SLABS = D_MODEL // LANES


def _load_tokens(ref, n):
    return jnp.concatenate([ref[pl.ds(s, n, stride=SLABS), :] for s in range(SLABS)], axis=1)


def _store_tokens(ref, val):
    n = val.shape[0]
    for s in range(SLABS):
        ref[pl.ds(s, n, stride=SLABS), :] = val[:, s * LANES:(s + 1) * LANES]


def _tiled_spec(rows, first=0):
    return pl.BlockSpec((SLABS * rows, LANES), lambda i, *_: (i + first, 0))


def _row_tiled_body(xa_ref, xb_ref, o_ref, *, tiles_a):
    i = pl.program_id(0)

    @pl.when(i < tiles_a)
    def _():
        _store_tokens(o_ref, xa_ref[...])

    @pl.when(i >= tiles_a)
    def _():
        _store_tokens(o_ref, xb_ref[...])


def _row_tiled_concat(xa, xb):
    tiles_a = xa.shape[0] // ROW_TILE
    tiles_b = xb.shape[0] // ROW_TILE
    n = xa.shape[0] + xb.shape[0]
    return pl.pallas_call(
        functools.partial(_row_tiled_body, tiles_a=tiles_a),
        grid=(tiles_a + tiles_b,),
        in_specs=[pl.BlockSpec((ROW_TILE, D_MODEL), lambda i: (jnp.minimum(i, tiles_a - 1), 0)),
                  pl.BlockSpec((ROW_TILE, D_MODEL), lambda i: (jnp.maximum(i - tiles_a, 0), 0))],
        out_specs=_tiled_spec(ROW_TILE),
        out_shape=jax.ShapeDtypeStruct((SLABS * n, LANES), F32),
        compiler_params=_params("parallel"),
        name="row_tiled_concat",
    )(xa, xb)


CONV_ROWS = 64
CONV_LANES = 256


def _mixer_in_body(hp_ref, hn_ref, x_ref, g_ref, w_ref, cw_ref, cb_ref, lg_ref, lb_ref,
                   q_ref, k_ref, v_ref, c_ref, win_ref, u_ref, *, n_tiles):
    i = pl.program_id(0)
    prev_tile = jnp.maximum(i - 1, 0)
    this_tile = jnp.minimum(i, n_tiles - 1)

    @pl.when(i == 0)
    def _():
        win_ref[...] = jnp.zeros_like(win_ref)

    x = _load_tokens(x_ref, ROW_TILE)
    xn = (x * lax.rsqrt(jnp.mean(x * x, axis=-1, keepdims=True) + EPS) * g_ref[...]).astype(BF16)

    def proj(j, width):
        return jnp.dot(xn, w_ref[:, j:j + width], preferred_element_type=F32)

    val = proj(0, D_CONV)
    gate = proj(D_CONV, D_CONV)
    u_ref[...] = val * _sigmoid(gate)
    q_ref[...] = proj(2 * D_CONV, D_ATTN) * (HEAD_DIM ** -0.5 * LOG2E)
    k_ref[...] = proj(2 * D_CONV + D_ATTN, D_ATTN)
    v_ref[...] = proj(2 * D_CONV + 2 * D_ATTN, D_ATTN)

    body = CONV_PAD + ROW_TILE
    win_ref[body:, :] = u_ref[0:CONV_PAD, :] * (hn_ref[prev_tile] > 0).astype(F32)
    first = CONV_PAD - CONV_WIDTH // 2
    span = CONV_ROWS + 2 * CONV_PAD
    for r0 in range(0, ROW_TILE, CONV_ROWS):
        halves = []
        for c0 in range(0, D_CONV, CONV_LANES):
            win = win_ref[r0:r0 + span, c0:c0 + CONV_LANES]
            acc = jnp.zeros((CONV_ROWS, CONV_LANES), F32)
            for phase in range(8):
                taps = [k for k in range(CONV_WIDTH) if (first + k) % 8 == phase]
                if not taps:
                    continue
                shifted = win if phase == 0 else pltpu.roll(win, span - phase, axis=0)
                for k in taps:
                    off = first + k - phase
                    acc = acc + shifted[off:off + CONV_ROWS, :] * cw_ref[k:k + 1, c0:c0 + CONV_LANES]
            halves.append(acc)
        y = jnp.concatenate(halves, axis=1) + cb_ref[...]
        mu = jnp.mean(y, axis=-1, keepdims=True)
        yc = y - mu
        var = jnp.mean(yc * yc, axis=-1, keepdims=True)
        yn = yc * lax.rsqrt(var + EPS) * lg_ref[...] + lb_ref[...]
        c_ref[r0:r0 + CONV_ROWS, :] = (yn * _sigmoid(yn)).astype(BF16)

    win_ref[0:CONV_PAD, :] = win_ref[ROW_TILE:body, :] * (hp_ref[this_tile] > 0).astype(F32)
    win_ref[CONV_PAD:body, :] = u_ref[...]


def _mixer_in(xt, tile_prev, tile_next, g, w_in_bf16, conv_w, conv_b, ln_g, ln_b):
    n = xt.shape[0] // SLABS
    n_tiles = n // ROW_TILE
    cur = lambda i, hp, hn: (jnp.minimum(i, n_tiles - 1), 0)
    full = lambda a: pl.BlockSpec(a.shape, lambda i, hp, hn: (0,) * a.ndim)
    row = pl.BlockSpec((ROW_TILE, D_ATTN), cur)
    grid_spec = pltpu.PrefetchScalarGridSpec(
        num_scalar_prefetch=2,
        grid=(n_tiles + 1,),
        in_specs=[pl.BlockSpec((SLABS * ROW_TILE, LANES), cur), full(g), full(w_in_bf16),
                  full(conv_w), full(conv_b), full(ln_g), full(ln_b)],
        out_specs=[row, row, row,
                   pl.BlockSpec((ROW_TILE, D_CONV), lambda i, hp, hn: (jnp.maximum(i - 1, 0), 0))],
        scratch_shapes=[pltpu.VMEM((ROW_TILE + 2 * CONV_PAD, D_CONV), F32), pltpu.VMEM((ROW_TILE, D_CONV), F32)],
    )
    return pl.pallas_call(
        functools.partial(_mixer_in_body, n_tiles=n_tiles),
        grid_spec=grid_spec,
        out_shape=[jax.ShapeDtypeStruct((n, D_ATTN), F32)] * 3 + [jax.ShapeDtypeStruct((n, D_CONV), BF16)],
        compiler_params=_params("arbitrary"),
        name="mixer_in_conv",
    )(tile_prev, tile_next, xt, g, w_in_bf16, conv_w, conv_b, ln_g, ln_b)


def _t5_bucket(rel):
    half = N_BUCKETS // 2
    max_exact = half // 2
    ret = np.where(rel > 0, half, 0)
    n = np.abs(rel)
    large = max_exact + (np.log(np.maximum(n, 1) / max_exact)
                         / np.log(MAX_DISTANCE / max_exact) * (half - max_exact)).astype(np.int32)
    large = np.minimum(large, half - 1)
    return (ret + np.where(n < max_exact, n, large)).astype(np.int32)


def _bias_tables(rel_bias):
    delta = np.arange(QGROUP + KGROUP - 1) - (QGROUP - 1) - ATTN_HALF
    band = np.abs(delta) <= ATTN_HALF
    tabs = []
    for d in ATTN_DILATIONS:
        onehot = np.eye(N_BUCKETS, dtype=np.float32)[_t5_bucket(d * delta)]
        base = jnp.dot(onehot, rel_bias.astype(F32), precision=lax.Precision.HIGHEST)
        base = jnp.where(band[:, None], base * LOG2E, NEG)
        rows = [base[QGROUP - 1 - i:QGROUP - 1 - i + KGROUP] for i in range(QGROUP)]
        b = jnp.transpose(jnp.stack(rows, axis=0), (2, 0, 1))
        tabs.append(b.reshape(N_HEADS // 2, 2 * QGROUP, KGROUP))
    return jnp.stack(tabs, axis=0)


MERGE_ROWS = 256


def _attn_body(hp_ref, hn_ref, q_ref, kp_ref, kc_ref, kn_ref, vp_ref, vc_ref, vn_ref, tab_ref, o_ref,
               kw_ref, vw_ref, *part_refs):
    nb = len(ATTN_DILATIONS)
    ob_refs, mb_refs, lb_refs = part_refs[:nb], part_refs[nb:2 * nb], part_refs[2 * nb:]
    c = pl.program_id(0)
    has_prev = hp_ref[c] > 0
    has_next = hn_ref[c] > 0
    kw_ref[0:HALO, :] = kp_ref[...]
    kw_ref[HALO:HALO + CHUNK, :] = kc_ref[...]
    kw_ref[HALO + CHUNK:, :] = kn_ref[...]
    vw_ref[0:HALO, :] = vp_ref[...]
    vw_ref[HALO:HALO + CHUNK, :] = vc_ref[...]
    vw_ref[HALO + CHUNK:, :] = vn_ref[...]

    first_head = lax.broadcasted_iota(I32, (QGROUP, LANES), 1) < HEAD_DIM
    key_pos = lax.broadcasted_iota(I32, (1, KGROUP), 1)

    for bi, d in enumerate(ATTN_DILATIONS):
        stride = None if d == 1 else d
        for g in range(CHUNK // QGROUP):
            res, blk = g % d, g // d
            q0 = blk * (QGROUP * d) + res
            k0 = HALO + (blk * QGROUP - ATTN_HALF) * d + res
            k_last = k0 + d * (KGROUP - 1)
            q = q_ref[pl.ds(q0, QGROUP, stride=stride), :]
            k = kw_ref[pl.ds(k0, KGROUP, stride=stride), :]
            v = vw_ref[pl.ds(k0, KGROUP, stride=stride), :]
            q2 = jnp.concatenate([jnp.where(first_head, q, 0.0), jnp.where(first_head, 0.0, q)], axis=0)
            s = lax.dot_general(q2.astype(BF16), k.astype(BF16), (((1,), (1,)), ((), ())),
                                preferred_element_type=F32)
            s = s + tab_ref[bi, 0]
            row = k0 + d * key_pos
            if k0 < HALO:
                s = jnp.where((row >= HALO) | has_prev, s, NEG)
            if k_last >= HALO + CHUNK:
                s = jnp.where((row < HALO + CHUNK) | has_next, s, NEG)
            m = jnp.max(s, axis=-1, keepdims=True)
            p = jnp.exp2(s - m)
            l = jnp.sum(p, axis=-1, keepdims=True)
            o = jnp.dot(p.astype(BF16), v.astype(BF16), preferred_element_type=F32)
            rows = pl.ds(q0, QGROUP, stride=stride)
            ob_refs[bi][rows, :] = jnp.where(first_head, o[:QGROUP], o[QGROUP:])
            mb_refs[bi][rows, :] = jnp.where(first_head, m[:QGROUP], m[QGROUP:])
            lb_refs[bi][rows, :] = jnp.where(first_head, l[:QGROUP], l[QGROUP:])

    def merge(t, carry):
        rows = pl.ds(pl.multiple_of(t * MERGE_ROWS, MERGE_ROWS), MERGE_ROWS)
        ms = [r[rows, :] for r in mb_refs]
        m_tot = functools.reduce(jnp.maximum, ms)
        ws = [jnp.exp2(m - m_tot) for m in ms]
        num = sum(w * r[rows, :] for w, r in zip(ws, ob_refs))
        den = sum(w * r[rows, :] for w, r in zip(ws, lb_refs))
        o_ref[rows, :] = (num / den).astype(BF16)
        return carry

    lax.fori_loop(0, CHUNK // MERGE_ROWS, merge, 0)


def _dilated_attention(q, k, v, has_prev, has_next, tables):
    n = q.shape[0]
    n_chunks = n // CHUNK
    per = CHUNK // HALO
    last = n // HALO - 1
    cur = pl.BlockSpec((CHUNK, LANES), lambda c, h, hp, hn: (c, h))
    prev = pl.BlockSpec((HALO, LANES), lambda c, h, hp, hn: (jnp.maximum(c * per - 1, 0), h))
    nxt = pl.BlockSpec((HALO, LANES), lambda c, h, hp, hn: (jnp.minimum((c + 1) * per, last), h))
    grid_spec = pltpu.PrefetchScalarGridSpec(
        num_scalar_prefetch=2,
        grid=(n_chunks, D_ATTN // LANES),
        in_specs=[cur, prev, cur, nxt, prev, cur, nxt,
                  pl.BlockSpec((len(ATTN_DILATIONS), 1, 2 * QGROUP, KGROUP), lambda c, h, hp, hn: (0, h, 0, 0))],
        out_specs=cur,
        scratch_shapes=([pltpu.VMEM((CHUNK + 2 * HALO, LANES), F32)] * 2
                        + [pltpu.VMEM((CHUNK, LANES), F32)] * (3 * len(ATTN_DILATIONS))),
    )
    return pl.pallas_call(
        _attn_body,
        grid_spec=grid_spec,
        out_shape=jax.ShapeDtypeStruct((n, D_ATTN), BF16),
        compiler_params=_params("parallel", "parallel"),
        name="dilated_attention",
    )(has_prev, has_next, q, k, k, k, v, v, v, tables)


def _mixer_out_body(cv_ref, at_ref, x_ref, w_ref, g_ref, wr_ref, xo_ref, h_ref, aff_ref):
    y = (jnp.dot(cv_ref[...], w_ref[0:D_CONV, :], preferred_element_type=F32)
         + jnp.dot(at_ref[...], w_ref[D_CONV:, :], preferred_element_type=F32))
    x = _load_tokens(x_ref, ROW_TILE) + y
    _store_tokens(xo_ref, x)
    h = x * lax.rsqrt(jnp.mean(x * x, axis=-1, keepdims=True) + EPS) * g_ref[...]
    _store_tokens(h_ref, h)
    h_hi = h.astype(BF16)
    h_lo = (h - h_hi.astype(F32)).astype(BF16)
    both = jnp.dot(h_hi, wr_ref[...], preferred_element_type=F32)
    low = jnp.dot(h_lo, wr_ref[:, :LANES], preferred_element_type=F32)
    logits = both[:, :LANES] + both[:, LANES:] + low
    logits = jnp.transpose(logits)[:N_EXPERTS, :]
    e = jnp.exp(logits - jnp.max(logits, axis=0, keepdims=True))
    aff_ref[...] = e / jnp.sum(e, axis=0, keepdims=True)


def _mixer_out(conv_out, attn_out, xt, w_out_bf16, g, w_router_pad):
    n = xt.shape[0] // SLABS
    row = lambda w: pl.BlockSpec((ROW_TILE, w), lambda i: (i, 0))
    full = lambda a: pl.BlockSpec(a.shape, lambda i: (0,) * a.ndim)
    tiled = jax.ShapeDtypeStruct(xt.shape, F32)
    return pl.pallas_call(
        _mixer_out_body,
        grid=(n // ROW_TILE,),
        in_specs=[row(D_CONV), row(D_ATTN), _tiled_spec(ROW_TILE), full(w_out_bf16), full(g), full(w_router_pad)],
        out_specs=[_tiled_spec(ROW_TILE), _tiled_spec(ROW_TILE),
                   pl.BlockSpec((N_EXPERTS, ROW_TILE), lambda i: (0, i))],
        out_shape=[tiled, tiled, jax.ShapeDtypeStruct((N_EXPERTS, n), F32)],
        input_output_aliases={2: 0},
        compiler_params=_params("parallel"),
        name="mixer_out_router",
    )(conv_out, attn_out, xt, w_out_bf16, g, w_router_pad)


def _count(mask):
    return jnp.sum(jnp.sum(mask.astype(F32), axis=0, keepdims=True), axis=1, keepdims=True)


def _threshold_body(aff_ref, thr_ref, *, cap):
    thr = [jnp.zeros((1, 1), I32) for _ in range(N_EXPERTS)]
    for bit in range(29, -1, -1):
        for e in range(N_EXPERTS):
            bits = pltpu.bitcast(aff_ref[e], I32)
            cand = thr[e] | (1 << bit)
            thr[e] = jnp.where(_count(bits >= cand) >= cap, cand, thr[e])
    for e in range(N_EXPERTS):
        thr_ref[e] = jnp.broadcast_to(thr[e], (1, LANES))


def _select_body(aff_ref, thr_ref, idx_ref, gate_ref, *, cap, tok0):
    a = aff_ref[0]
    segs = a.shape[0]
    bits = pltpu.bitcast(a, I32)
    count = _count
    thr = thr_ref[0][:, 0:1]
    above = bits > thr
    equal = bits == thr
    need = cap - count(above)

    lane_r = lax.broadcasted_iota(I32, (LANES, LANES), 0)
    lane_c = lax.broadcasted_iota(I32, (LANES, LANES), 1)
    upper = (lane_r <= lane_c).astype(BF16)
    seg_r = lax.broadcasted_iota(I32, (segs, segs), 0)
    seg_c = lax.broadcasted_iota(I32, (segs, segs), 1)
    before = (seg_c < seg_r).astype(BF16)

    def prefix(mask):
        incl = jnp.dot(mask.astype(BF16), upper, preferred_element_type=F32)
        tot = jnp.broadcast_to(incl[:, LANES - 1:LANES], (segs, LANES))
        pre = jnp.dot(before, tot.astype(BF16), preferred_element_type=F32)
        return incl, pre, tot

    eq_f = equal.astype(F32)
    incl_e, pre_e, _ = prefix(eq_f)
    chosen = above | (equal & (pre_e + incl_e - eq_f < need))
    incl, pre, tot = prefix(chosen.astype(F32))
    seg_lo = pre[:, 0:1]
    seg_hi = seg_lo + tot[:, 0:1]
    incl_t = jnp.transpose(incl).astype(BF16)
    aff_t = jnp.transpose(a)
    lane_id = lax.broadcasted_iota(I32, (LANES, SLOT_BLOCK), 0).astype(F32)

    for jb in range(cap // SLOT_BLOCK):
        slot = (jb * SLOT_BLOCK + lax.broadcasted_iota(I32, (1, SLOT_BLOCK), 1)).astype(F32)
        seg_of = jnp.sum((seg_hi <= slot).astype(F32), axis=0, keepdims=True)
        onehot = ((seg_lo <= slot) & (slot < seg_hi)).astype(F32)
        rank = slot - jnp.sum(onehot * seg_lo, axis=0, keepdims=True)
        counts = jnp.dot(incl_t, onehot.astype(BF16), preferred_element_type=F32)
        lane_of = jnp.sum((counts <= rank).astype(F32), axis=0, keepdims=True)
        token = (seg_of * LANES + lane_of).astype(I32) + tok0
        idx_ref[0, :, jb * SLOT_BLOCK:(jb + 1) * SLOT_BLOCK] = token * SLABS
        vals = jnp.dot(aff_t, onehot, precision=lax.Precision.HIGHEST, preferred_element_type=F32)
        gate_ref[0, :, jb * SLOT_BLOCK:(jb + 1) * SLOT_BLOCK] = jnp.sum(
            jnp.where(lane_id == lane_of, vals, 0.0), axis=0, keepdims=True)


def _select(aff_group, cap, tok0):
    n = aff_group.shape[1]
    aff3 = aff_group.reshape(N_EXPERTS, n // LANES, LANES)
    thr = pl.pallas_call(
        functools.partial(_threshold_body, cap=cap),
        out_shape=jax.ShapeDtypeStruct((N_EXPERTS, 1, LANES), I32),
        compiler_params=pltpu.CompilerParams(vmem_limit_bytes=VMEM_LIMIT),
        name="expert_threshold",
    )(aff3)
    spec = pl.BlockSpec((1, 1, cap), lambda e: (e, 0, 0))
    return pl.pallas_call(
        functools.partial(_select_body, cap=cap, tok0=tok0),
        grid=(N_EXPERTS,),
        in_specs=[pl.BlockSpec((1, n // LANES, LANES), lambda e: (e, 0, 0)),
                  pl.BlockSpec((1, 1, LANES), lambda e: (e, 0, 0))],
        out_specs=[spec, spec],
        out_shape=[jax.ShapeDtypeStruct((N_EXPERTS, 1, cap), I32), jax.ShapeDtypeStruct((N_EXPERTS, 1, cap), F32)],
        compiler_params=_params("parallel"),
        name="expert_select",
    )(aff3, thr)


def _ffn_body(idx_ref, h_hbm, gate_ref, wg_ref, wu_ref, wd_ref, x_in_hbm, x_hbm,
              hbuf_ref, xbuf_ref, xb_ref, xb_next_ref, acc_ref, sem_h, sem_x, sem_s, *, rows, n_tiles, steps):
    del x_in_hbm
    g = pl.program_id(0)
    f = pl.program_id(1)
    n_early = steps - 3
    per_early = rows // n_early
    base = g * rows
    base_next = jnp.minimum(g + 1, n_tiles - 1) * rows
    base_prev = jnp.maximum(g - 1, 0) * rows

    def hbm_tok(hbm, tile_base, j):
        return hbm.at[pl.ds(pl.multiple_of(idx_ref[tile_base + j], SLABS), SLABS), :]

    def buf_tok(buf, j):
        return buf.at[pl.ds(pl.multiple_of(j * SLABS, SLABS), SLABS), :]

    def h_gather(tile_base, j):
        return pltpu.make_async_copy(hbm_tok(h_hbm, tile_base, j), buf_tok(hbuf_ref, j), sem_h)

    def x_gather(tile_base, j):
        return pltpu.make_async_copy(hbm_tok(x_hbm, tile_base, j), buf_tok(xbuf_ref, j), sem_x)

    def x_scatter(tile_base, j):
        return pltpu.make_async_copy(buf_tok(xbuf_ref, j), hbm_tok(x_hbm, tile_base, j), sem_s)

    def wait_h():
        pltpu.make_async_copy(h_hbm.at[pl.ds(0, SLABS * rows), :], hbuf_ref, sem_h).wait()

    def wait_x():
        pltpu.make_async_copy(x_hbm.at[pl.ds(0, SLABS * rows), :], xbuf_ref, sem_x).wait()

    def wait_s():
        pltpu.make_async_copy(xbuf_ref, x_hbm.at[pl.ds(0, SLABS * rows), :], sem_s).wait()

    def stage_h():
        for s in range(SLABS):
            slab = hbuf_ref[pl.ds(s, rows, stride=SLABS), :]
            xb_next_ref[:, s * LANES:(s + 1) * LANES] = slab.astype(BF16)

    def matmuls(first=False):
        xb = xb_ref[...]
        hg = jnp.dot(xb, wg_ref[0, 0].astype(BF16), preferred_element_type=F32)
        hu = jnp.dot(xb, wu_ref[0, 0].astype(BF16), preferred_element_type=F32)
        hid = (hg * _sigmoid(hg) * hu).astype(BF16)
        y = jnp.dot(hid, wd_ref[0, 0].astype(BF16), preferred_element_type=F32)
        if first:
            acc_ref[...] = y
        else:
            acc_ref[...] += y

    @pl.when((g == 0) & (f == 0))
    def _():
        def issue(j, carry):
            h_gather(0, j).start()
            x_gather(0, j).start()
            return carry
        lax.fori_loop(0, rows, issue, 0)
        wait_h()
        wait_x()
        stage_h()

    def early(first):
        j0 = f * per_early
        for i in range(per_early):
            h_gather(base_next, j0 + i).start()
            x_scatter(base_prev, j0 + i).start()
        if first:
            xb_ref[...] = xb_next_ref[...]
        matmuls(first)

    @pl.when(f == 0)
    def _():
        early(True)

    @pl.when((f > 0) & (f < n_early))
    def _():
        early(False)

    @pl.when(f == n_early)
    def _():
        matmuls()
        wait_s()
        wait_h()

    @pl.when(f == n_early + 1)
    def _():
        for i in range(rows):
            x_gather(base, i).start()
        stage_h()
        matmuls()

    @pl.when(f == steps - 1)
    def _():
        matmuls()
        wait_x()
        gate = gate_ref[0]
        for s in range(SLABS):
            slab = pl.ds(s, rows, stride=SLABS)
            xbuf_ref[slab, :] = xbuf_ref[slab, :] + acc_ref[:, s * LANES:(s + 1) * LANES] * gate

    @pl.when((g == n_tiles - 1) & (f == steps - 1))
    def _():
        def issue(j, carry):
            x_scatter(base, j).start()
            return carry
        lax.fori_loop(0, rows, issue, 0)
        wait_s()


def _expert_ffn(idx_flat, h, gates, w_gate, w_up, w_down, layer, x):
    n_slots = gates.shape[1]
    d_exp = w_gate.shape[3]
    tiles = n_slots // FFN_ROWS
    steps = d_exp // FFN_COLS
    assert steps >= 4 and FFN_ROWS % (steps - 3) == 0 and FFN_ROWS % 2 == 0
    any_spec = pl.BlockSpec(memory_space=pl.ANY)
    grid_spec = pltpu.PrefetchScalarGridSpec(
        num_scalar_prefetch=1,
        grid=(N_EXPERTS * tiles, steps),
        in_specs=[
            any_spec,
            pl.BlockSpec((1, FFN_ROWS, 1), lambda g, f, idx: (g // tiles, g % tiles, 0)),
            pl.BlockSpec((1, 1, D_MODEL, FFN_COLS), lambda g, f, idx: (layer, g // tiles, 0, f)),
            pl.BlockSpec((1, 1, D_MODEL, FFN_COLS), lambda g, f, idx: (layer, g // tiles, 0, f)),
            pl.BlockSpec((1, 1, FFN_COLS, D_MODEL), lambda g, f, idx: (layer, g // tiles, f, 0)),
            any_spec,
        ],
        out_specs=any_spec,
        scratch_shapes=[pltpu.VMEM((SLABS * FFN_ROWS, LANES), F32), pltpu.VMEM((SLABS * FFN_ROWS, LANES), F32),
                        pltpu.VMEM((FFN_ROWS, D_MODEL), BF16), pltpu.VMEM((FFN_ROWS, D_MODEL), BF16),
                        pltpu.VMEM((FFN_ROWS, D_MODEL), F32),
                        pltpu.SemaphoreType.DMA, pltpu.SemaphoreType.DMA, pltpu.SemaphoreType.DMA],
    )
    return pl.pallas_call(
        functools.partial(_ffn_body, rows=FFN_ROWS, n_tiles=N_EXPERTS * tiles, steps=steps),
        grid_spec=grid_spec,
        out_shape=jax.ShapeDtypeStruct(x.shape, F32),
        input_output_aliases={6: 0},
        compiler_params=_params("arbitrary", "arbitrary"),
        name="expert_ffn",
    )(idx_flat, h, gates, w_gate, w_up, w_down, x)


def _final_norm_body(x_ref, g_ref, o_ref):
    x = _load_tokens(x_ref, ROW_TILE)
    o_ref[...] = x * lax.rsqrt(jnp.mean(x * x, axis=-1, keepdims=True) + EPS) * g_ref[...]


def _final_norm(xt, g, tok0, n):
    return pl.pallas_call(
        _final_norm_body,
        grid=(n // ROW_TILE,),
        in_specs=[_tiled_spec(ROW_TILE, tok0 // ROW_TILE), pl.BlockSpec(g.shape, lambda i: (0, 0))],
        out_specs=pl.BlockSpec((ROW_TILE, D_MODEL), lambda i: (i, 0)),
        out_shape=jax.ShapeDtypeStruct((n, D_MODEL), F32),
        compiler_params=_params("parallel"),
        name="final_norm",
    )(xt, g)


def _neighbours(seq_lens, size):
    has_prev, has_next = [], []
    for length in seq_lens:
        assert length % size == 0
        k = length // size
        has_prev += [int(i > 0) for i in range(k)]
        has_next += [int(i < k - 1) for i in range(k)]
    return jnp.asarray(has_prev, I32), jnp.asarray(has_next, I32)


def _trunk(x, groups, norm_mix_g, w_in, conv_w, conv_b, conv_ln_g, conv_ln_b, w_out, rel_bias,
           norm_ffn_g, w_router, w_gate, w_up, w_down, norm_final_g):
    seq_lens = [s for b, s in groups for _ in range(b)]
    has_prev, has_next = _neighbours(seq_lens, CHUNK)
    tile_prev, tile_next = _neighbours(seq_lens, ROW_TILE)
    tables = _bias_tables(rel_bias)
    depth = w_in.shape[0]
    row2 = lambda a: a.reshape(1, -1)
    for l in range(depth):
        q, k, v, conv_out = _mixer_in(x, tile_prev, tile_next, row2(norm_mix_g[l]), w_in[l].astype(BF16),
                                      conv_w[l], row2(conv_b[l]), row2(conv_ln_g[l]), row2(conv_ln_b[l]))
        attn_out = _dilated_attention(q, k, v, has_prev, has_next, tables)
        wr = jnp.pad(w_router[l], ((0, 0), (0, LANES - N_EXPERTS)))
        wr_hi = wr.astype(BF16)
        wr_lo = (wr - wr_hi.astype(F32)).astype(BF16)
        x, h, aff = _mixer_out(conv_out, attn_out, x, w_out[l].astype(BF16), row2(norm_ffn_g[l]),
                               jnp.concatenate([wr_hi, wr_lo], axis=1))
        idx, gates, tok0 = [], [], 0
        for b, s in groups:
            n = b * s
            cap = CAPACITY_FACTOR * n // N_EXPERTS
            i, g = _select(aff[:, tok0:tok0 + n], cap, tok0)
            idx.append(i[:, 0, :])
            gates.append(g[:, 0, :])
            tok0 += n
        idx = jnp.concatenate(idx, axis=1)
        gates = jnp.concatenate(gates, axis=1)[:, :, None]
        x = _expert_ffn(idx.reshape(-1), h, gates, w_gate, w_up, w_down, l, x)
    outs, tok0 = [], 0
    for b, s in groups:
        outs.append(_final_norm(x, row2(norm_final_g), tok0, b * s).reshape(b, s, D_MODEL))
        tok0 += b * s
    return outs


def kernel(x_prompt, x_sample, norm_mix_g, w_in, conv_w, conv_b, conv_ln_g, conv_ln_b, w_out, rel_bias,
           norm_ffn_g, w_router, w_gate, w_up, w_down, norm_final_g):
    groups = [x_prompt.shape[:2], x_sample.shape[:2]]
    x = _row_tiled_concat(x_prompt.reshape(-1, D_MODEL), x_sample.reshape(-1, D_MODEL))
    y_prompt, y_sample = _trunk(x, groups, norm_mix_g, w_in, conv_w, conv_b, conv_ln_g, conv_ln_b, w_out,
                                rel_bias, norm_ffn_g, w_router, w_gate, w_up, w_down, norm_final_g)
    return y_prompt, y_sample
```
